```python
import jax, jax.numpy as jnp
from jax import lax
import numpy as np

D_MODEL = 1024
BATCH = 16
SEQ = 4096
DEPTH = 1
DEC_BATCH = 8
DEC_SEQ = 32
PAST_LEN = 1024

CHUNK = 64
N_HEADS = 16
HEAD_DIM = D_MODEL // N_HEADS
ATT_DIM = N_HEADS * HEAD_DIM
CONV_DIM = D_MODEL
CONV_GROUPS = 16
CONV_WIDTH = 3
D_FF = ((8 * D_MODEL // 3 + 255) // 256) * 256
Q_BLOCK = 128
N_MOD = 6
EPS = 1e-6
SPLITS = [ATT_DIM, 2 * ATT_DIM, 3 * ATT_DIM,
          3 * ATT_DIM + CONV_DIM, 3 * ATT_DIM + 2 * CONV_DIM, 3 * ATT_DIM + 3 * CONV_DIM,
          3 * ATT_DIM + 3 * CONV_DIM + D_MODEL]
IN_DIM = 3 * ATT_DIM + 3 * CONV_DIM + 2 * D_MODEL

kernel_name = "hybrid_stickbreak_shortconv_convffn_step"


def rms_norm(x, gain):
    x32 = x.astype(jnp.float32)
    y = x32 * lax.rsqrt(jnp.mean(x32 * x32, axis=-1, keepdims=True) + EPS)
    return (y * gain.astype(jnp.float32)).astype(x.dtype)


def modulate(xn, shift, scale):
    return xn * (1 + scale[:, None, :]) + shift[:, None, :]


def causal_dwconv(u, prev, w, b):
    t = u.shape[1]
    u_pad = jnp.concatenate([prev.astype(u.dtype), u], axis=1)
    y = sum((w[i] * u_pad[:, i:i + t] for i in range(CONV_WIDTH)), b)
    return y, u_pad[:, u_pad.shape[1] - (CONV_WIDTH - 1):]


def stick_breaking(q, k, v, q_pos, k_pos):
    z = jnp.einsum("bqhd,bkhd->bhqk", q.astype(jnp.float32), k.astype(jnp.float32)) * (HEAD_DIM ** -0.5)
    mask = k_pos[None, :] < q_pos[:, None]
    log_keep = jnp.where(mask, jax.nn.log_sigmoid(-z), 0.0)
    suffix = lax.cumsum(log_keep, axis=3, reverse=True) - log_keep
    a = jnp.where(mask, jnp.exp(jax.nn.log_sigmoid(z) + suffix), 0.0)
    o = jnp.einsum("bhqk,bkhd->bqhd", a, v.astype(jnp.float32))
    return o.astype(v.dtype)


def stick_breaking_blocked(q, k, v):
    b, t, h, d = q.shape
    nb = t // Q_BLOCK
    qb = q.reshape(b, nb, Q_BLOCK, h, d).transpose(1, 0, 2, 3, 4)
    pb = jnp.arange(t).reshape(nb, Q_BLOCK)
    k_pos = jnp.arange(t)
    ob = lax.map(lambda a: stick_breaking(a[0], k, v, a[1], k_pos), (qb, pb))
    return ob.transpose(1, 0, 2, 3, 4).reshape(b, t, h, d)


def trunk_layer(x, c, past_k, past_v, conv_prev, ffn_prev,
                w_ada, b_ada, g_norm1, w_in, w_conv, b_conv, w_branch_a, w_branch_b, w_out,
                g_norm2, w_up, w_fconv, b_fconv, w_down):
    bsz, t, _ = x.shape
    mod = jax.nn.silu(c) @ w_ada + b_ada
    shift1, scale1, gate1, shift2, scale2, gate2 = jnp.split(mod, N_MOD, axis=-1)

    h = modulate(rms_norm(x, g_norm1), shift1, scale1)
    q, k, v, b_gate, c_gate, u_in, g_a, g_b = jnp.split(h @ w_in, SPLITS, axis=-1)
    q = q.reshape(bsz, t, N_HEADS, HEAD_DIM)
    k = k.reshape(bsz, t, N_HEADS, HEAD_DIM)
    v = v.reshape(bsz, t, N_HEADS, HEAD_DIM)
    if past_k is None:
        o_a = stick_breaking_blocked(q, k, v)
    else:
        p = past_k.shape[1]
        k_all = jnp.concatenate([past_k.astype(k.dtype), k], axis=1)
        v_all = jnp.concatenate([past_v.astype(v.dtype), v], axis=1)
        o_a = stick_breaking(q, k_all, v_all, p + jnp.arange(t), jnp.arange(p + t))
    conv_out, new_conv = causal_dwconv(c_gate * u_in, conv_prev, w_conv, b_conv)
    o_b = b_gate * conv_out
    merged = (jax.nn.sigmoid(g_a) * (o_a.reshape(bsz, t, ATT_DIM) @ w_branch_a)
              + jax.nn.sigmoid(g_b) * (o_b @ w_branch_b))
    x = x + gate1[:, None, :] * (merged @ w_out)

    h2 = modulate(rms_norm(x, g_norm2), shift2, scale2)
    up, gate = jnp.split(h2 @ w_up, 2, axis=-1)
    up_c, new_ffn = causal_dwconv(up, ffn_prev, w_fconv, b_fconv)
    x = x + gate2[:, None, :] * ((jax.nn.silu(up_c) * gate) @ w_down)
    return x, k, v, new_conv, new_ffn


def setup_inputs(seed: int = 0) -> dict:
    key = jax.random.key(seed)
    ks = iter(jax.random.split(key, 32))
    f32 = jnp.float32

    def nrm(shape, scale=1.0):
        return jax.random.normal(next(ks), shape, f32) * scale

    def gain(shape):
        return 1.0 + nrm(shape, 0.02)

    return {
        "x_prompt": nrm((BATCH, SEQ, D_MODEL)),
        "x_sample": nrm((DEC_BATCH, DEC_SEQ, D_MODEL)),
        "cache_k": nrm((DEPTH, DEC_BATCH, PAST_LEN, N_HEADS, HEAD_DIM)),
        "cache_v": nrm((DEPTH, DEC_BATCH, PAST_LEN, N_HEADS, HEAD_DIM)),
        "state_conv": nrm((DEPTH, DEC_BATCH, CONV_WIDTH - 1, CONV_DIM)),
        "state_ffn_conv": nrm((DEPTH, DEC_BATCH, CONV_WIDTH - 1, D_FF)),
        "c_prompt": nrm((BATCH, D_MODEL)),
        "c_sample": nrm((DEC_BATCH, D_MODEL)),
        "w_ada": nrm((DEPTH, D_MODEL, N_MOD * D_MODEL), 0.5 * D_MODEL ** -0.5),
        "b_ada": nrm((DEPTH, N_MOD * D_MODEL), 0.01),
        "g_norm1": gain((DEPTH, D_MODEL)),
        "w_in": nrm((DEPTH, D_MODEL, IN_DIM), D_MODEL ** -0.5),
        "w_conv": nrm((DEPTH, CONV_WIDTH, CONV_DIM), CONV_WIDTH ** -0.5),
        "b_conv": nrm((DEPTH, CONV_DIM), 0.01),
        "w_branch_a": nrm((DEPTH, ATT_DIM, D_MODEL), ATT_DIM ** -0.5),
        "w_branch_b": nrm((DEPTH, CONV_DIM, D_MODEL), CONV_DIM ** -0.5),
        "w_out": nrm((DEPTH, D_MODEL, D_MODEL), D_MODEL ** -0.5),
        "g_norm2": gain((DEPTH, D_MODEL)),
        "w_up": nrm((DEPTH, D_MODEL, 2 * D_FF), D_MODEL ** -0.5),
        "w_fconv": nrm((DEPTH, CONV_WIDTH, D_FF), CONV_WIDTH ** -0.5),
        "b_fconv": nrm((DEPTH, D_FF), 0.01),
        "w_down": nrm((DEPTH, D_FF, D_MODEL), D_FF ** -0.5),
        "g_final": gain((D_MODEL,)),
    }


def reference(x_prompt, x_sample, cache_k, cache_v, state_conv, state_ffn_conv, c_prompt, c_sample,
              w_ada, b_ada, g_norm1, w_in, w_conv, b_conv, w_branch_a, w_branch_b, w_out,
              g_norm2, w_up, w_fconv, b_fconv, w_down, g_final):
    bp = x_prompt.shape[0]
    xp, xs = x_prompt, x_sample
    kp, vp, cp, fp, ks_, vs_, cs_, fs_ = [], [], [], [], [], [], [], []
    for l in range(DEPTH):
        lw = (w_ada[l], b_ada[l], g_norm1[l], w_in[l], w_conv[l], b_conv[l], w_branch_a[l],
              w_branch_b[l], w_out[l], g_norm2[l], w_up[l], w_fconv[l], b_fconv[l], w_down[l])
        zero_conv = jnp.zeros((bp, CONV_WIDTH - 1, CONV_DIM), xp.dtype)
        zero_ffn = jnp.zeros((bp, CONV_WIDTH - 1, D_FF), xp.dtype)
        xp, k1, v1, c1, f1 = trunk_layer(xp, c_prompt, None, None, zero_conv, zero_ffn, *lw)
        xs, k2, v2, c2, f2 = trunk_layer(xs, c_sample, cache_k[l], cache_v[l], state_conv[l],
                                         state_ffn_conv[l], *lw)
        kp.append(k1); vp.append(v1); cp.append(c1); fp.append(f1)
        ks_.append(k2); vs_.append(v2); cs_.append(c2); fs_.append(f2)
    y_prompt = rms_norm(xp, g_final)
    y_sample = rms_norm(xs, g_final)
    return (y_prompt, y_sample,
            jnp.stack(kp), jnp.stack(vp), jnp.stack(cp), jnp.stack(fp),
            jnp.stack(ks_), jnp.stack(vs_), jnp.stack(cs_), jnp.stack(fs_))
```

```python
import functools

import jax
import jax.numpy as jnp
from jax import lax
from jax.experimental import pallas as pl
from jax.experimental.pallas import tpu as pltpu

F32 = jnp.float32
BF16 = jnp.bfloat16

D_MODEL = 1024
N_HEADS = 16
HEAD_DIM = 64
N_SEG = 8
N_MOD = 6
CONV_WIDTH = 3
EPS = 1e-6
LANES = 128
SUBLANES = 8
ATTN_BLOCK = 256
VMEM_LIMIT = 56 * 1024 * 1024


def _params(n_axes):
    return pltpu.CompilerParams(dimension_semantics=("arbitrary",) * n_axes,
                                vmem_limit_bytes=VMEM_LIMIT)


def _sigmoid(x):
    return 1.0 / (1.0 + jnp.exp(-x))


def _rms_modulate(x, gain, shift, scale):
    y = x * lax.rsqrt(jnp.mean(x * x, axis=-1, keepdims=True) + EPS) * gain
    return y * (1.0 + scale) + shift


def _causal_conv(u, pad_ref, w_ref, b_ref, first, prev_ref, carry_ref, new_ref, j):
    tm = u.shape[0]
    pad_ref[SUBLANES:SUBLANES + tm, :] = u

    @pl.when(first)
    def _():
        pad_ref[SUBLANES - 2:SUBLANES, :] = prev_ref[0]

    @pl.when(jnp.logical_not(first))
    def _():
        pad_ref[SUBLANES - 2:SUBLANES, :] = carry_ref[j]

    tail = u[tm - 2:tm, :]
    carry_ref[j] = tail
    new_ref[0] = tail
    y = b_ref[...] + w_ref[0:1, :] * pad_ref[SUBLANES - 2:SUBLANES - 2 + tm, :]
    y = y + w_ref[1:2, :] * pad_ref[SUBLANES - 1:SUBLANES - 1 + tm, :]
    return y + w_ref[2:3, :] * u


def _ada_kernel(c_ref, w_ref, b_ref, o_ref):
    c = c_ref[...]
    o_ref[...] = jnp.dot(c * _sigmoid(c), w_ref[...], preferred_element_type=F32) + b_ref[...]


def _ada(c, w_ada, b_ada):
    n, tn = c.shape[0], 1536
    return pl.pallas_call(
        _ada_kernel,
        grid=(N_MOD * D_MODEL // tn,),
        in_specs=[pl.BlockSpec((n, D_MODEL), lambda j: (0, 0)),
                  pl.BlockSpec((D_MODEL, tn), lambda j: (0, j)),
                  pl.BlockSpec((1, tn), lambda j: (0, j))],
        out_specs=pl.BlockSpec((n, tn), lambda j: (0, j)),
        out_shape=jax.ShapeDtypeStruct((n, N_MOD * D_MODEL), F32),
        compiler_params=_params(1),
        name="ada",
    )(c, w_ada, b_ada.reshape(1, -1))


def _inproj_kernel(x_ref, mod_ref, g_ref, w_ref, wc_ref, bc_ref, prev_ref,
                   q_ref, kf_ref, vf_ref, kb_ref, vb_ref, ob_ref, ga_ref, gb_ref, nc_ref,
                   h_scr, pad_scr, carry_scr, *, tiles_per_seq):
    i, j = pl.program_id(0), pl.program_id(1)

    @pl.when(j == 0)
    def _():
        h = _rms_modulate(x_ref[...], g_ref[...], mod_ref[0, 0:1, :], mod_ref[0, 1:2, :])
        h_scr[...] = h.astype(BF16)

    h = h_scr[...]

    def proj(s):
        return jnp.dot(h, w_ref[s], preferred_element_type=F32)

    q_ref[...] = (proj(0) * (HEAD_DIM ** -0.5)).astype(BF16)
    k = proj(1)
    kf_ref[...] = k
    kb_ref[...] = k.astype(BF16)
    v = proj(2)
    vf_ref[...] = v
    vb_ref[...] = v.astype(BF16)
    ga_ref[...] = _sigmoid(proj(6)).astype(BF16)
    gb_ref[...] = _sigmoid(proj(7)).astype(BF16)
    cu = proj(4) * proj(5)
    first = (i % tiles_per_seq) == 0
    conv = _causal_conv(cu, pad_scr, wc_ref, bc_ref, first, prev_ref, carry_scr, nc_ref, j)
    ob_ref[...] = (proj(3) * conv).astype(BF16)


def _inproj(x2d, mod, g1, w_seg, w_conv, b_conv, prev, seq_len, tm, tc):
    m = x2d.shape[0]
    n_seq = m // seq_len
    tiles_per_seq = seq_len // tm
    n_ct = D_MODEL // tc
    row = lambda i, j: (i, 0)
    tile = lambda i, j: (i, j)
    seq = lambda i, j: (i // tiles_per_seq, 0, j)
    act = lambda dt: jax.ShapeDtypeStruct((m, D_MODEL), dt)
    return pl.pallas_call(
        functools.partial(_inproj_kernel, tiles_per_seq=tiles_per_seq),
        grid=(m // tm, n_ct),
        in_specs=[pl.BlockSpec((tm, D_MODEL), row),
                  pl.BlockSpec((1, N_MOD, D_MODEL), lambda i, j: (i // tiles_per_seq, 0, 0)),
                  pl.BlockSpec((1, D_MODEL), lambda i, j: (0, 0)),
                  pl.BlockSpec((N_SEG, D_MODEL, tc), lambda i, j: (0, 0, j)),
                  pl.BlockSpec((CONV_WIDTH, tc), lambda i, j: (0, j)),
                  pl.BlockSpec((1, tc), lambda i, j: (0, j)),
                  pl.BlockSpec((1, CONV_WIDTH - 1, tc), seq)],
        out_specs=[pl.BlockSpec((tm, tc), tile)] * 8
        + [pl.BlockSpec((1, CONV_WIDTH - 1, tc), lambda i, j: (i, 0, j))],
        out_shape=[act(BF16), act(F32), act(F32), act(BF16), act(BF16), act(BF16), act(BF16),
                   act(BF16), jax.ShapeDtypeStruct((m // tm, CONV_WIDTH - 1, D_MODEL), F32)],
        scratch_shapes=[pltpu.VMEM((tm, D_MODEL), BF16),
                        pltpu.VMEM((tm + SUBLANES, tc), F32),
                        pltpu.VMEM((n_ct, CONV_WIDTH - 1, tc), F32)],
        compiler_params=_params(2),
        name="inproj",
    )(x2d, mod, g1, w_seg, w_conv, b_conv, prev)


def _attn_block(qm_ref, kblk, vblk, tri, valid):
    lo = lax.broadcasted_iota(jnp.int32, (1, LANES), 1) < HEAD_DIM
    tots, pvs = [], []
    for hh in range(2):
        z = lax.dot_general(qm_ref[hh], kblk, (((1,), (1,)), ((), ())),
                            preferred_element_type=F32)
        lk = -(jnp.maximum(z, 0.0) + jnp.log(1.0 + jnp.exp(-jnp.abs(z))))
        if valid is not None:
            lk = jnp.where(valid, lk, 0.0)
        s = jnp.dot(lk.astype(BF16), tri, preferred_element_type=F32)
        a = jnp.exp(z + s)
        if valid is not None:
            a = jnp.where(valid, a, 0.0)
        pvs.append(jnp.dot(a.astype(BF16), vblk, preferred_element_type=F32))
        tots.append(jnp.sum(lk, axis=1, keepdims=True))
    return jnp.where(lo, tots[0], tots[1]), jnp.where(lo, pvs[0], pvs[1])


def _attn_kernel(q_ref, k_ref, v_ref, tri_ref, o_ref, qm_scr, acc_scr, carry_scr, *, q_off):
    bq, bk = q_ref.shape[1], ATTN_BLOCK
    qi = pl.program_id(2)
    q_lo = q_off + qi * bq
    j_diag = q_lo // bk
    lo = lax.broadcasted_iota(jnp.int32, (1, LANES), 1) < HEAD_DIM
    q = q_ref[0]
    qm_scr[0] = jnp.where(lo, q, jnp.zeros_like(q))
    qm_scr[1] = jnp.where(lo, jnp.zeros_like(q), q)
    tri = tri_ref[...]

    def kv(jb):
        start = pl.multiple_of(jb * bk, bk)
        return k_ref[0, pl.ds(start, bk), :], v_ref[0, pl.ds(start, bk), :]

    kblk, vblk = kv(j_diag)
    k_pos = j_diag * bk + lax.broadcasted_iota(jnp.int32, (bq, bk), 1)
    q_pos = q_lo + lax.broadcasted_iota(jnp.int32, (bq, bk), 0)
    tot, pv = _attn_block(qm_scr, kblk, vblk, tri, k_pos < q_pos)
    acc_scr[...] = pv
    carry_scr[...] = tot

    def older(it, _):
        kblk, vblk = kv(j_diag - 1 - it)
        tot, pv = _attn_block(qm_scr, kblk, vblk, tri, None)
        carry = carry_scr[...]
        acc_scr[...] += jnp.exp(carry) * pv
        carry_scr[...] = carry + tot
        return 0

    lax.fori_loop(0, j_diag, older, 0)
    o_ref[0] = acc_scr[...].astype(o_ref.dtype)


def _attention(q, k, v, q_off):
    b, tq, _ = q.shape
    tk = k.shape[1]
    bq = min(tq, ATTN_BLOCK)
    assert tq % bq == 0 and tk % ATTN_BLOCK == 0 and ATTN_BLOCK % bq == 0
    assert q_off % ATTN_BLOCK + tq <= ATTN_BLOCK or q_off % ATTN_BLOCK == 0
    assert q_off + tq <= tk
    r = lax.broadcasted_iota(jnp.int32, (ATTN_BLOCK, ATTN_BLOCK), 0)
    c = lax.broadcasted_iota(jnp.int32, (ATTN_BLOCK, ATTN_BLOCK), 1)
    tri = (r >= c).astype(BF16)
    return pl.pallas_call(
        functools.partial(_attn_kernel, q_off=q_off),
        grid=(b, D_MODEL // LANES, tq // bq),
        in_specs=[pl.BlockSpec((1, bq, LANES), lambda bi, hp, qi: (bi, qi, hp)),
                  pl.BlockSpec((1, tk, LANES), lambda bi, hp, qi: (bi, 0, hp)),
                  pl.BlockSpec((1, tk, LANES), lambda bi, hp, qi: (bi, 0, hp)),
                  pl.BlockSpec((ATTN_BLOCK, ATTN_BLOCK), lambda bi, hp, qi: (0, 0))],
        out_specs=pl.BlockSpec((1, bq, LANES), lambda bi, hp, qi: (bi, qi, hp)),
        out_shape=jax.ShapeDtypeStruct(q.shape, BF16),
        scratch_shapes=[pltpu.VMEM((2, bq, LANES), BF16),
                        pltpu.VMEM((bq, LANES), F32),
                        pltpu.VMEM((bq, LANES), F32)],
        compiler_params=_params(3),
        name="attn",
    )(q, k, v, tri)


def _mix_kernel(oa_ref, ob_ref, ga_ref, gb_ref, x_ref, mod_ref, wa_ref, wb_ref, wo_ref, g_ref,
                x1_ref, h2_ref):
    ya = jnp.dot(oa_ref[...], wa_ref[...], preferred_element_type=F32)
    yb = jnp.dot(ob_ref[...], wb_ref[...], preferred_element_type=F32)
    merged = ga_ref[...].astype(F32) * ya + gb_ref[...].astype(F32) * yb
    y = jnp.dot(merged.astype(BF16), wo_ref[...], preferred_element_type=F32)
    x1 = x_ref[...] + mod_ref[0, 2:3, :] * y
    x1_ref[...] = x1
    h2_ref[...] = _rms_modulate(x1, g_ref[...], mod_ref[0, 3:4, :], mod_ref[0, 4:5, :]).astype(BF16)


def _mix(oa, ob, ga, gb, x2d, mod, wa, wb, wo, g2, seq_len, tm):
    m = x2d.shape[0]
    tiles_per_seq = seq_len // tm
    row = pl.BlockSpec((tm, D_MODEL), lambda i: (i, 0))
    full = pl.BlockSpec((D_MODEL, D_MODEL), lambda i: (0, 0))
    return pl.pallas_call(
        _mix_kernel,
        grid=(m // tm,),
        in_specs=[row, row, row, row, row,
                  pl.BlockSpec((1, N_MOD, D_MODEL), lambda i: (i // tiles_per_seq, 0, 0)),
                  full, full, full, pl.BlockSpec((1, D_MODEL), lambda i: (0, 0))],
        out_specs=[row, row],
        out_shape=[jax.ShapeDtypeStruct((m, D_MODEL), F32), jax.ShapeDtypeStruct((m, D_MODEL), BF16)],
        compiler_params=_params(1),
        name="mix",
    )(oa, ob, ga, gb, x2d, mod, wa, wb, wo, g2)


def _ffn_kernel(h_ref, x_ref, mod_ref, wu_ref, wf_ref, bf_ref, prev_ref, wd_ref, g_ref,
                y_ref, nf_ref, acc_scr, pad_scr, carry_scr, *, tiles_per_seq):
    i, j = pl.program_id(0), pl.program_id(1)
    h = h_ref[...]
    up = jnp.dot(h, wu_ref[0], preferred_element_type=F32)
    gate = jnp.dot(h, wu_ref[1], preferred_element_type=F32)
    first = (i % tiles_per_seq) == 0
    up_c = _causal_conv(up, pad_scr, wf_ref, bf_ref, first, prev_ref, carry_scr, nf_ref, j)
    act = (up_c * _sigmoid(up_c) * gate).astype(BF16)
    part = jnp.dot(act, wd_ref[...], preferred_element_type=F32)

    @pl.when(j == 0)
    def _():
        acc_scr[...] = part

    @pl.when(j > 0)
    def _():
        acc_scr[...] += part

    @pl.when(j == pl.num_programs(1) - 1)
    def _():
        x2 = x_ref[...] + mod_ref[0, 5:6, :] * acc_scr[...]
        y_ref[...] = x2 * lax.rsqrt(jnp.mean(x2 * x2, axis=-1, keepdims=True) + EPS) * g_ref[...]


def _ffn(h2, x1, mod, w_up2, w_fconv, b_fconv, prev, w_down, g_final, seq_len, tm, tf):
    m = x1.shape[0]
    d_ff = w_down.shape[0]
    n_seq = m // seq_len
    tiles_per_seq = seq_len // tm
    n_ft = d_ff // tf
    row = pl.BlockSpec((tm, D_MODEL), lambda i, j: (i, 0))
    seq = lambda i, j: (i // tiles_per_seq, 0, j)
    return pl.pallas_call(
        functools.partial(_ffn_kernel, tiles_per_seq=tiles_per_seq),
        grid=(m // tm, n_ft),
        in_specs=[row, row,
                  pl.BlockSpec((1, N_MOD, D_MODEL), lambda i, j: (i // tiles_per_seq, 0, 0)),
                  pl.BlockSpec((2, D_MODEL, tf), lambda i, j: (0, 0, j)),
                  pl.BlockSpec((CONV_WIDTH, tf), lambda i, j: (0, j)),
                  pl.BlockSpec((1, tf), lambda i, j: (0, j)),
                  pl.BlockSpec((1, CONV_WIDTH - 1, tf), seq),
                  pl.BlockSpec((tf, D_MODEL), lambda i, j: (j, 0)),
                  pl.BlockSpec((1, D_MODEL), lambda i, j: (0, 0))],
        out_specs=[row, pl.BlockSpec((1, CONV_WIDTH - 1, tf), lambda i, j: (i, 0, j))],
        out_shape=[jax.ShapeDtypeStruct((m, D_MODEL), F32),
                   jax.ShapeDtypeStruct((m // tm, CONV_WIDTH - 1, d_ff), F32)],
        scratch_shapes=[pltpu.VMEM((tm, D_MODEL), F32),
                        pltpu.VMEM((tm + SUBLANES, tf), F32),
                        pltpu.VMEM((n_ft, CONV_WIDTH - 1, tf), F32)],
        compiler_params=_params(2),
        name="ffn",
    )(h2, x1, mod, w_up2, w_fconv, b_fconv, prev, w_down, g_final)


def _trunk(x, mod, past_k, past_v, conv_prev, ffn_prev, w, tm):
    bsz, t, _ = x.shape
    x2d = x.reshape(bsz * t, D_MODEL)
    q, kf, vf, kb, vb, ob, ga, gb, new_conv = _inproj(
        x2d, mod, w["g1"], w["w_seg"], w["w_conv"], w["b_conv"], conv_prev, t, tm, 256)
    shape3 = (bsz, t, D_MODEL)
    if past_k is None:
        k_all, v_all, q_off = kb.reshape(shape3), vb.reshape(shape3), 0
    else:
        q_off = past_k.shape[1]
        pad = -(q_off + t) % ATTN_BLOCK
        cat = lambda past, new: jnp.pad(
            jnp.concatenate([past.reshape(bsz, q_off, D_MODEL).astype(BF16), new.reshape(shape3)], axis=1),
            ((0, 0), (0, pad), (0, 0)))
        k_all, v_all = cat(past_k, kb), cat(past_v, vb)
    oa = _attention(q.reshape(shape3), k_all, v_all, q_off).reshape(bsz * t, D_MODEL)
    x1, h2 = _mix(oa, ob, ga, gb, x2d, mod, w["wa"], w["wb"], w["wo"], w["g2"], t, tm)
    y, new_ffn = _ffn(h2, x1, mod, w["w_up2"], w["w_fconv"], w["b_fconv"], ffn_prev, w["w_down"],
                      w["g_final"], t, tm, 256)
    heads = (1, bsz, t, N_HEADS, HEAD_DIM)
    last = lambda s: s.reshape(bsz, t // tm, CONV_WIDTH - 1, -1)[None, :, -1]
    return (y.reshape(bsz, t, D_MODEL), kf.reshape(heads), vf.reshape(heads),
            last(new_conv), last(new_ffn))


def kernel(x_prompt, x_sample, cache_k, cache_v, state_conv, state_ffn_conv, c_prompt, c_sample,
           w_ada, b_ada, g_norm1, w_in, w_conv, b_conv, w_branch_a, w_branch_b, w_out,
           g_norm2, w_up, w_fconv, b_fconv, w_down, g_final):
    assert w_ada.shape[0] == 1, "one trunk layer"
    bp, bs = x_prompt.shape[0], x_sample.shape[0]
    d_ff = w_down.shape[1]
    w = {
        "g1": g_norm1[0].reshape(1, D_MODEL),
        "w_seg": w_in[0].astype(BF16).reshape(D_MODEL, N_SEG, D_MODEL).transpose(1, 0, 2),
        "w_conv": w_conv[0], "b_conv": b_conv[0].reshape(1, -1),
        "wa": w_branch_a[0].astype(BF16), "wb": w_branch_b[0].astype(BF16),
        "wo": w_out[0].astype(BF16),
        "g2": g_norm2[0].reshape(1, D_MODEL),
        "w_up2": w_up[0].astype(BF16).reshape(D_MODEL, 2, d_ff).transpose(1, 0, 2),
        "w_fconv": w_fconv[0], "b_fconv": b_fconv[0].reshape(1, -1),
        "w_down": w_down[0].astype(BF16),
        "g_final": g_final.reshape(1, D_MODEL),
    }
    mod = _ada(jnp.concatenate([c_prompt, c_sample], axis=0), w_ada[0], b_ada[0])
    mod = mod.reshape(bp + bs, N_MOD, D_MODEL)
    zeros = lambda width: jnp.zeros((bp, CONV_WIDTH - 1, width), x_prompt.dtype)
    yp, kp, vp, cp, fp = _trunk(x_prompt, mod[:bp], None, None, zeros(D_MODEL), zeros(d_ff), w, 512)
    ys, ks, vs, cs, fs = _trunk(x_sample, mod[bp:], cache_k[0], cache_v[0], state_conv[0],
                                state_ffn_conv[0], w, x_sample.shape[1])
    return (yp, ys, kp, vp, cp, fp, ks, vs, cs, fs)
```

```python
import functools

import jax
import jax.numpy as jnp
from jax import lax
from jax.experimental import pallas as pl
from jax.experimental.pallas import tpu as pltpu

F32 = jnp.float32
BF16 = jnp.bfloat16

D_MODEL = 1024
N_HEADS = 16
HEAD_DIM = 64
N_SEG = 8
N_MOD = 6
CONV_WIDTH = 3
EPS = 1e-6
LANES = 128
SUBLANES = 8
ATTN_BLOCK = 256
VMEM_LIMIT = 56 * 1024 * 1024
DEAD_LOG_WEIGHT = -105.0


def _params(n_axes):
    return pltpu.CompilerParams(dimension_semantics=("arbitrary",) * n_axes,
                                vmem_limit_bytes=VMEM_LIMIT)


def _sigmoid(x):
    return 1.0 / (1.0 + jnp.exp(-x))


def _rms_modulate(x, gain, shift, scale):
    y = x * lax.rsqrt(jnp.mean(x * x, axis=-1, keepdims=True) + EPS) * gain
    return y * (1.0 + scale) + shift


def _causal_conv(u, pad_ref, w_ref, b_ref, first, prev_ref, carry_ref, new_ref, j):
    tm = u.shape[0]
    pad_ref[SUBLANES:SUBLANES + tm, :] = u

    @pl.when(first)
    def _():
        pad_ref[SUBLANES - 2:SUBLANES, :] = prev_ref[0]

    @pl.when(jnp.logical_not(first))
    def _():
        pad_ref[SUBLANES - 2:SUBLANES, :] = carry_ref[j]

    tail = u[tm - 2:tm, :]
    carry_ref[j] = tail
    new_ref[0] = tail
    y = b_ref[...] + w_ref[0:1, :] * pad_ref[SUBLANES - 2:SUBLANES - 2 + tm, :]
    y = y + w_ref[1:2, :] * pad_ref[SUBLANES - 1:SUBLANES - 1 + tm, :]
    return y + w_ref[2:3, :] * u


def _ada_kernel(c_ref, w_ref, b_ref, o_ref):
    c = c_ref[...]
    o_ref[...] = jnp.dot(c * _sigmoid(c), w_ref[...], preferred_element_type=F32) + b_ref[...]


def _ada(c, w_ada, b_ada):
    n, tn = c.shape[0], 1536
    return pl.pallas_call(
        _ada_kernel,
        grid=(N_MOD * D_MODEL // tn,),
        in_specs=[pl.BlockSpec((n, D_MODEL), lambda j: (0, 0)),
                  pl.BlockSpec((D_MODEL, tn), lambda j: (0, j)),
                  pl.BlockSpec((1, tn), lambda j: (0, j))],
        out_specs=pl.BlockSpec((n, tn), lambda j: (0, j)),
        out_shape=jax.ShapeDtypeStruct((n, N_MOD * D_MODEL), F32),
        compiler_params=_params(1),
        name="ada",
    )(c, w_ada, b_ada.reshape(1, -1))


def _inproj_kernel(x_ref, mod_ref, g_ref, w_ref, wc_ref, bc_ref, prev_ref,
                   q_ref, kf_ref, vf_ref, kb_ref, vb_ref, ob_ref, ga_ref, gb_ref, nc_ref,
                   h_scr, pad_scr, carry_scr, *, tiles_per_seq):
    i, j = pl.program_id(0), pl.program_id(1)

    @pl.when(j == 0)
    def _():
        h = _rms_modulate(x_ref[...], g_ref[...], mod_ref[0, 0:1, :], mod_ref[0, 1:2, :])
        h_scr[...] = h.astype(BF16)

    h = h_scr[...]

    def proj(s):
        return jnp.dot(h, w_ref[s], preferred_element_type=F32)

    q_ref[...] = (proj(0) * (HEAD_DIM ** -0.5)).astype(BF16)
    k = proj(1)
    kf_ref[...] = k
    kb_ref[...] = k.astype(BF16)
    v = proj(2)
    vf_ref[...] = v
    vb_ref[...] = v.astype(BF16)
    ga_ref[...] = _sigmoid(proj(6)).astype(BF16)
    gb_ref[...] = _sigmoid(proj(7)).astype(BF16)
    cu = proj(4) * proj(5)
    first = (i % tiles_per_seq) == 0
    conv = _causal_conv(cu, pad_scr, wc_ref, bc_ref, first, prev_ref, carry_scr, nc_ref, j)
    ob_ref[...] = (proj(3) * conv).astype(BF16)


def _inproj(x2d, mod, g1, w_seg, w_conv, b_conv, prev, seq_len, tm, tc):
    m = x2d.shape[0]
    n_seq = m // seq_len
    tiles_per_seq = seq_len // tm
    n_ct = D_MODEL // tc
    row = lambda i, j: (i, 0)
    tile = lambda i, j: (i, j)
    seq = lambda i, j: (i // tiles_per_seq, 0, j)
    act = lambda dt: jax.ShapeDtypeStruct((m, D_MODEL), dt)
    return pl.pallas_call(
        functools.partial(_inproj_kernel, tiles_per_seq=tiles_per_seq),
        grid=(m // tm, n_ct),
        in_specs=[pl.BlockSpec((tm, D_MODEL), row),
                  pl.BlockSpec((1, N_MOD, D_MODEL), lambda i, j: (i // tiles_per_seq, 0, 0)),
                  pl.BlockSpec((1, D_MODEL), lambda i, j: (0, 0)),
                  pl.BlockSpec((N_SEG, D_MODEL, tc), lambda i, j: (0, 0, j)),
                  pl.BlockSpec((CONV_WIDTH, tc), lambda i, j: (0, j)),
                  pl.BlockSpec((1, tc), lambda i, j: (0, j)),
                  pl.BlockSpec((1, CONV_WIDTH - 1, tc), seq)],
        out_specs=[pl.BlockSpec((tm, tc), tile)] * 8
        + [pl.BlockSpec((1, CONV_WIDTH - 1, tc), lambda i, j: (i, 0, j))],
        out_shape=[act(BF16), act(F32), act(F32), act(BF16), act(BF16), act(BF16), act(BF16),
                   act(BF16), jax.ShapeDtypeStruct((m // tm, CONV_WIDTH - 1, D_MODEL), F32)],
        scratch_shapes=[pltpu.VMEM((tm, D_MODEL), BF16),
                        pltpu.VMEM((tm + SUBLANES, tc), F32),
                        pltpu.VMEM((n_ct, CONV_WIDTH - 1, tc), F32)],
        compiler_params=_params(2),
        name="inproj",
    )(x2d, mod, g1, w_seg, w_conv, b_conv, prev)


def _attn_block(qm_ref, kblk, vblk, tri, valid):
    lo = lax.broadcasted_iota(jnp.int32, (1, LANES), 1) < HEAD_DIM
    tots, pvs = [], []
    for hh in range(2):
        z = lax.dot_general(qm_ref[hh], kblk, (((1,), (1,)), ((), ())),
                            preferred_element_type=F32)
        lk = -(jnp.maximum(z, 0.0) + jnp.log(1.0 + jnp.exp(-jnp.abs(z))))
        if valid is not None:
            lk = jnp.where(valid, lk, 0.0)
        s = jnp.dot(lk.astype(BF16), tri, preferred_element_type=F32)
        a = jnp.exp(z + s)
        if valid is not None:
            a = jnp.where(valid, a, 0.0)
        pvs.append(jnp.dot(a.astype(BF16), vblk, preferred_element_type=F32))
        tots.append(jnp.sum(lk, axis=1, keepdims=True))
    return jnp.where(lo, tots[0], tots[1]), jnp.where(lo, pvs[0], pvs[1])


def _attn_kernel(q_ref, k_ref, v_ref, tri_ref, o_ref, qm_scr, acc_scr, carry_scr, *, q_off):
    bq, bk = q_ref.shape[1], ATTN_BLOCK
    qi = pl.program_id(2)
    q_lo = q_off + qi * bq
    j_diag = q_lo // bk
    lo = lax.broadcasted_iota(jnp.int32, (1, LANES), 1) < HEAD_DIM
    q = q_ref[0]
    qm_scr[0] = jnp.where(lo, q, jnp.zeros_like(q))
    qm_scr[1] = jnp.where(lo, jnp.zeros_like(q), q)
    tri = tri_ref[...]

    def kv(jb):
        start = pl.multiple_of(jb * bk, bk)
        return k_ref[0, pl.ds(start, bk), :], v_ref[0, pl.ds(start, bk), :]

    kblk, vblk = kv(j_diag)
    k_pos = j_diag * bk + lax.broadcasted_iota(jnp.int32, (bq, bk), 1)
    q_pos = q_lo + lax.broadcasted_iota(jnp.int32, (bq, bk), 0)
    tot, pv = _attn_block(qm_scr, kblk, vblk, tri, k_pos < q_pos)
    acc_scr[...] = pv
    carry_scr[...] = tot

    def any_live(carry):
        m = jnp.max(jnp.max(carry, axis=0, keepdims=True), axis=1, keepdims=True)
        return m[0, 0] > DEAD_LOG_WEIGHT

    def older(state):
        it, _ = state
        kblk, vblk = kv(j_diag - 1 - it)
        tot, pv = _attn_block(qm_scr, kblk, vblk, tri, None)
        carry = carry_scr[...]
        acc_scr[...] += jnp.exp(carry) * pv
        carry = carry + tot
        carry_scr[...] = carry
        return it + 1, any_live(carry)

    lax.while_loop(lambda state: jnp.logical_and(state[0] < j_diag, state[1]), older,
                   (jnp.int32(0), any_live(tot)))
    o_ref[0] = acc_scr[...].astype(o_ref.dtype)


def _attention(q, k, v, q_off):
    b, tq, _ = q.shape
    tk = k.shape[1]
    bq = min(tq, ATTN_BLOCK)
    assert tq % bq == 0 and tk % ATTN_BLOCK == 0 and ATTN_BLOCK % bq == 0
    assert q_off % ATTN_BLOCK + tq <= ATTN_BLOCK or q_off % ATTN_BLOCK == 0
    assert q_off + tq <= tk
    r = lax.broadcasted_iota(jnp.int32, (ATTN_BLOCK, ATTN_BLOCK), 0)
    c = lax.broadcasted_iota(jnp.int32, (ATTN_BLOCK, ATTN_BLOCK), 1)
    tri = (r >= c).astype(BF16)
    return pl.pallas_call(
        functools.partial(_attn_kernel, q_off=q_off),
        grid=(b, D_MODEL // LANES, tq // bq),
        in_specs=[pl.BlockSpec((1, bq, LANES), lambda bi, hp, qi: (bi, qi, hp)),
                  pl.BlockSpec((1, tk, LANES), lambda bi, hp, qi: (bi, 0, hp)),
                  pl.BlockSpec((1, tk, LANES), lambda bi, hp, qi: (bi, 0, hp)),
                  pl.BlockSpec((ATTN_BLOCK, ATTN_BLOCK), lambda bi, hp, qi: (0, 0))],
        out_specs=pl.BlockSpec((1, bq, LANES), lambda bi, hp, qi: (bi, qi, hp)),
        out_shape=jax.ShapeDtypeStruct(q.shape, BF16),
        scratch_shapes=[pltpu.VMEM((2, bq, LANES), BF16),
                        pltpu.VMEM((bq, LANES), F32),
                        pltpu.VMEM((bq, LANES), F32)],
        compiler_params=_params(3),
        name="attn",
    )(q, k, v, tri)


def _mix_kernel(oa_ref, ob_ref, ga_ref, gb_ref, x_ref, mod_ref, wa_ref, wb_ref, wo_ref, g_ref,
                x1_ref, h2_ref):
    ya = jnp.dot(oa_ref[...], wa_ref[...], preferred_element_type=F32)
    yb = jnp.dot(ob_ref[...], wb_ref[...], preferred_element_type=F32)
    merged = ga_ref[...].astype(F32) * ya + gb_ref[...].astype(F32) * yb
    y = jnp.dot(merged.astype(BF16), wo_ref[...], preferred_element_type=F32)
    x1 = x_ref[...] + mod_ref[0, 2:3, :] * y
    x1_ref[...] = x1
    h2_ref[...] = _rms_modulate(x1, g_ref[...], mod_ref[0, 3:4, :], mod_ref[0, 4:5, :]).astype(BF16)


def _mix(oa, ob, ga, gb, x2d, mod, wa, wb, wo, g2, seq_len, tm):
    m = x2d.shape[0]
    tiles_per_seq = seq_len // tm
    row = pl.BlockSpec((tm, D_MODEL), lambda i: (i, 0))
    full = pl.BlockSpec((D_MODEL, D_MODEL), lambda i: (0, 0))
    return pl.pallas_call(
        _mix_kernel,
        grid=(m // tm,),
        in_specs=[row, row, row, row, row,
                  pl.BlockSpec((1, N_MOD, D_MODEL), lambda i: (i // tiles_per_seq, 0, 0)),
                  full, full, full, pl.BlockSpec((1, D_MODEL), lambda i: (0, 0))],
        out_specs=[row, row],
        out_shape=[jax.ShapeDtypeStruct((m, D_MODEL), F32), jax.ShapeDtypeStruct((m, D_MODEL), BF16)],
        compiler_params=_params(1),
        name="mix",
    )(oa, ob, ga, gb, x2d, mod, wa, wb, wo, g2)


def _ffn_kernel(h_ref, x_ref, mod_ref, wu_ref, wf_ref, bf_ref, prev_ref, wd_ref, g_ref,
                y_ref, nf_ref, acc_scr, pad_scr, carry_scr, *, tiles_per_seq):
    i, j = pl.program_id(0), pl.program_id(1)
    h = h_ref[...]
    up = jnp.dot(h, wu_ref[0], preferred_element_type=F32)
    gate = jnp.dot(h, wu_ref[1], preferred_element_type=F32)
    first = (i % tiles_per_seq) == 0
    up_c = _causal_conv(up, pad_scr, wf_ref, bf_ref, first, prev_ref, carry_scr, nf_ref, j)
    act = (up_c * _sigmoid(up_c) * gate).astype(BF16)
    part = jnp.dot(act, wd_ref[...], preferred_element_type=F32)

    @pl.when(j == 0)
    def _():
        acc_scr[...] = part

    @pl.when(j > 0)
    def _():
        acc_scr[...] += part

    @pl.when(j == pl.num_programs(1) - 1)
    def _():
        x2 = x_ref[...] + mod_ref[0, 5:6, :] * acc_scr[...]
        y_ref[...] = x2 * lax.rsqrt(jnp.mean(x2 * x2, axis=-1, keepdims=True) + EPS) * g_ref[...]


def _ffn(h2, x1, mod, w_up2, w_fconv, b_fconv, prev, w_down, g_final, seq_len, tm, tf):
    m = x1.shape[0]
    d_ff = w_down.shape[0]
    n_seq = m // seq_len
    tiles_per_seq = seq_len // tm
    n_ft = d_ff // tf
    row = pl.BlockSpec((tm, D_MODEL), lambda i, j: (i, 0))
    seq = lambda i, j: (i // tiles_per_seq, 0, j)
    return pl.pallas_call(
        functools.partial(_ffn_kernel, tiles_per_seq=tiles_per_seq),
        grid=(m // tm, n_ft),
        in_specs=[row, row,
                  pl.BlockSpec((1, N_MOD, D_MODEL), lambda i, j: (i // tiles_per_seq, 0, 0)),
                  pl.BlockSpec((2, D_MODEL, tf), lambda i, j: (0, 0, j)),
                  pl.BlockSpec((CONV_WIDTH, tf), lambda i, j: (0, j)),
                  pl.BlockSpec((1, tf), lambda i, j: (0, j)),
                  pl.BlockSpec((1, CONV_WIDTH - 1, tf), seq),
                  pl.BlockSpec((tf, D_MODEL), lambda i, j: (j, 0)),
                  pl.BlockSpec((1, D_MODEL), lambda i, j: (0, 0))],
        out_specs=[row, pl.BlockSpec((1, CONV_WIDTH - 1, tf), lambda i, j: (i, 0, j))],
        out_shape=[jax.ShapeDtypeStruct((m, D_MODEL), F32),
                   jax.ShapeDtypeStruct((m // tm, CONV_WIDTH - 1, d_ff), F32)],
        scratch_shapes=[pltpu.VMEM((tm, D_MODEL), F32),
                        pltpu.VMEM((tm + SUBLANES, tf), F32),
                        pltpu.VMEM((n_ft, CONV_WIDTH - 1, tf), F32)],
        compiler_params=_params(2),
        name="ffn",
    )(h2, x1, mod, w_up2, w_fconv, b_fconv, prev, w_down, g_final)


def _trunk(x, mod, past_k, past_v, conv_prev, ffn_prev, w, tm):
    bsz, t, _ = x.shape
    x2d = x.reshape(bsz * t, D_MODEL)
    q, kf, vf, kb, vb, ob, ga, gb, new_conv = _inproj(
        x2d, mod, w["g1"], w["w_seg"], w["w_conv"], w["b_conv"], conv_prev, t, tm, 256)
    shape3 = (bsz, t, D_MODEL)
    if past_k is None:
        k_all, v_all, q_off = kb.reshape(shape3), vb.reshape(shape3), 0
    else:
        q_off = past_k.shape[1]
        pad = -(q_off + t) % ATTN_BLOCK
        cat = lambda past, new: jnp.pad(
            jnp.concatenate([past.reshape(bsz, q_off, D_MODEL).astype(BF16), new.reshape(shape3)], axis=1),
            ((0, 0), (0, pad), (0, 0)))
        k_all, v_all = cat(past_k, kb), cat(past_v, vb)
    oa = _attention(q.reshape(shape3), k_all, v_all, q_off).reshape(bsz * t, D_MODEL)
    x1, h2 = _mix(oa, ob, ga, gb, x2d, mod, w["wa"], w["wb"], w["wo"], w["g2"], t, tm)
    y, new_ffn = _ffn(h2, x1, mod, w["w_up2"], w["w_fconv"], w["b_fconv"], ffn_prev, w["w_down"],
                      w["g_final"], t, tm, 256)
    heads = (1, bsz, t, N_HEADS, HEAD_DIM)
    last = lambda s: s.reshape(bsz, t // tm, CONV_WIDTH - 1, -1)[None, :, -1]
    return (y.reshape(bsz, t, D_MODEL), kf.reshape(heads), vf.reshape(heads),
            last(new_conv), last(new_ffn))


def kernel(x_prompt, x_sample, cache_k, cache_v, state_conv, state_ffn_conv, c_prompt, c_sample,
           w_ada, b_ada, g_norm1, w_in, w_conv, b_conv, w_branch_a, w_branch_b, w_out,
           g_norm2, w_up, w_fconv, b_fconv, w_down, g_final):
    assert w_ada.shape[0] == 1, "one trunk layer"
    bp, bs = x_prompt.shape[0], x_sample.shape[0]
    d_ff = w_down.shape[1]
    w = {
        "g1": g_norm1[0].reshape(1, D_MODEL),
        "w_seg": w_in[0].astype(BF16).reshape(D_MODEL, N_SEG, D_MODEL).transpose(1, 0, 2),
        "w_conv": w_conv[0], "b_conv": b_conv[0].reshape(1, -1),
        "wa": w_branch_a[0].astype(BF16), "wb": w_branch_b[0].astype(BF16),
        "wo": w_out[0].astype(BF16),
        "g2": g_norm2[0].reshape(1, D_MODEL),
        "w_up2": w_up[0].astype(BF16).reshape(D_MODEL, 2, d_ff).transpose(1, 0, 2),
        "w_fconv": w_fconv[0], "b_fconv": b_fconv[0].reshape(1, -1),
        "w_down": w_down[0].astype(BF16),
        "g_final": g_final.reshape(1, D_MODEL),
    }
    mod = _ada(jnp.concatenate([c_prompt, c_sample], axis=0), w_ada[0], b_ada[0])
    mod = mod.reshape(bp + bs, N_MOD, D_MODEL)
    zeros = lambda width: jnp.zeros((bp, CONV_WIDTH - 1, width), x_prompt.dtype)
    yp, kp, vp, cp, fp = _trunk(x_prompt, mod[:bp], None, None, zeros(D_MODEL), zeros(d_ff), w, 512)
    ys, ks, vs, cs, fs = _trunk(x_sample, mod[bp:], cache_k[0], cache_v[0], state_conv[0],
                                state_ffn_conv[0], w, x_sample.shape[1])
    return (yp, ys, kp, vp, cp, fp, ks, vs, cs, fs)
```

```python
import functools
import math

import jax
import jax.numpy as jnp
from jax import lax
from jax.experimental import pallas as pl
from jax.experimental.pallas import tpu as pltpu

F32 = jnp.float32
BF16 = jnp.bfloat16

D_MODEL = 1024
N_HEADS = 16
HEAD_DIM = 64
N_SEG = 8
N_MOD = 6
CONV_WIDTH = 3
EPS = 1e-6
LANES = 128
SUBLANES = 8
MXU_WIDTH = 256
ATTN_BLOCK = 256
VMEM_LIMIT = 56 * 1024 * 1024
LOG2E = math.log2(math.e)
Q_SCALE = HEAD_DIM ** -0.5 * LOG2E
DEAD_LOG2_DECAY = 105.0 * LOG2E
CAUSAL_BIAS = -1e30


def _params(n_axes):
    return pltpu.CompilerParams(dimension_semantics=("arbitrary",) * n_axes,
                                vmem_limit_bytes=VMEM_LIMIT)


def _resident(shape):
    return pl.BlockSpec(shape, lambda *_: (0,) * len(shape), pipeline_mode=pl.Buffered(1))


def _sigmoid(x):
    return 1.0 / (1.0 + jnp.exp(-x))


def _rms_modulate(x, gain, shift, scale):
    y = x * lax.rsqrt(jnp.mean(x * x, axis=-1, keepdims=True) + EPS) * gain
    return y * (1.0 + scale) + shift


def _shift_rows(u, fill):
    r = pltpu.roll(u, 1, 0)
    is_row0 = lax.broadcasted_iota(jnp.int32, (SUBLANES, u.shape[1]), 0) == 0
    top = jnp.where(is_row0, fill, r[:SUBLANES])
    return jnp.concatenate([top, r[SUBLANES:]], axis=0)


def _causal_conv(u, hist, w, b):
    u1 = _shift_rows(u, hist[1:2])
    u2 = _shift_rows(u1, hist[0:1])
    return b + w[0:1] * u2 + w[1:2] * u1 + w[2:3] * u


def _ada_kernel(c_ref, w_ref, b_ref, o_ref):
    c = c_ref[...]
    o_ref[...] = jnp.dot(c * _sigmoid(c), w_ref[...], preferred_element_type=F32) + b_ref[...]


def _ada(c, w_ada, b_ada):
    n, tn = c.shape[0], 1536
    return pl.pallas_call(
        _ada_kernel,
        grid=(N_MOD * D_MODEL // tn,),
        in_specs=[pl.BlockSpec((n, D_MODEL), lambda j: (0, 0)),
                  pl.BlockSpec((D_MODEL, tn), lambda j: (0, j)),
                  pl.BlockSpec((1, tn), lambda j: (0, j))],
        out_specs=pl.BlockSpec((n, tn), lambda j: (0, j)),
        out_shape=jax.ShapeDtypeStruct((n, N_MOD * D_MODEL), F32),
        compiler_params=_params(1),
        name="ada",
    )(c, w_ada, b_ada.reshape(1, -1))


def _inproj_kernel(x_ref, mod_ref, g_ref, w_ref, wc_ref, bc_ref, prev_ref,
                   q_ref, kf_ref, vf_ref, kb_ref, vb_ref, ob_ref, ga_ref, gb_ref, nc_ref,
                   h_scr, hist_scr, *, tiles_per_seq):
    tm = x_ref.shape[0]

    @pl.when(pl.program_id(0) % tiles_per_seq == 0)
    def _():
        hist_scr[...] = prev_ref[0]

    h = _rms_modulate(x_ref[...], g_ref[...], mod_ref[0, 0:1, :], mod_ref[0, 1:2, :])
    h_scr[...] = h.astype(BF16)

    for c in range(D_MODEL // MXU_WIDTH):
        sl = slice(c * MXU_WIDTH, (c + 1) * MXU_WIDTH)

        def proj(s):
            return jnp.dot(h_scr[...], w_ref[s, :, sl], preferred_element_type=F32)

        q_ref[:, sl] = (proj(0) * Q_SCALE).astype(BF16)
        k = proj(1)
        kf_ref[:, sl] = k
        kb_ref[:, sl] = k.astype(BF16)
        v = proj(2)
        vf_ref[:, sl] = v
        vb_ref[:, sl] = v.astype(BF16)
        ga_ref[:, sl] = _sigmoid(proj(6)).astype(BF16)
        gb_ref[:, sl] = _sigmoid(proj(7)).astype(BF16)
        cu = proj(4) * proj(5)
        conv = _causal_conv(cu, hist_scr[:, sl], wc_ref[:, sl], bc_ref[:, sl])
        ob_ref[:, sl] = (proj(3) * conv).astype(BF16)
        tail = cu[tm - 2:tm, :]
        hist_scr[:, sl] = tail
        nc_ref[0, :, sl] = tail


def _inproj(x2d, mod, g1, w_seg, w_conv, b_conv, prev, seq_len, tm):
    m = x2d.shape[0]
    tiles_per_seq = seq_len // tm
    row = pl.BlockSpec((tm, D_MODEL), lambda i: (i, 0))
    act = lambda dt: jax.ShapeDtypeStruct((m, D_MODEL), dt)
    state = (1, CONV_WIDTH - 1, D_MODEL)
    return pl.pallas_call(
        functools.partial(_inproj_kernel, tiles_per_seq=tiles_per_seq),
        grid=(m // tm,),
        in_specs=[row,
                  pl.BlockSpec((1, N_MOD, D_MODEL), lambda i: (i // tiles_per_seq, 0, 0)),
                  _resident((1, D_MODEL)),
                  _resident((N_SEG, D_MODEL, D_MODEL)),
                  _resident((CONV_WIDTH, D_MODEL)),
                  _resident((1, D_MODEL)),
                  pl.BlockSpec(state, lambda i: (i // tiles_per_seq, 0, 0))],
        out_specs=[row] * 8 + [pl.BlockSpec(state, lambda i: (i, 0, 0))],
        out_shape=[act(BF16), act(F32), act(F32), act(BF16), act(BF16), act(BF16), act(BF16),
                   act(BF16), jax.ShapeDtypeStruct((m // tm, CONV_WIDTH - 1, D_MODEL), F32)],
        scratch_shapes=[pltpu.VMEM((tm, D_MODEL), BF16),
                        pltpu.VMEM((CONV_WIDTH - 1, D_MODEL), F32)],
        compiler_params=_params(1),
        name="inproj",
    )(x2d, mod, g1, w_seg, w_conv, b_conv, prev)


def _attn_block(qs, kblk, vblk, tri, causal):
    bq = qs.shape[0] // 2
    lo = lax.broadcasted_iota(jnp.int32, (1, LANES), 1) < HEAD_DIM
    z = lax.dot_general(qs, kblk, (((1,), (1,)), ((), ())), preferred_element_type=F32)
    sp = jnp.maximum(z, 0.0) + jnp.log(1.0 + jnp.exp2(-jnp.abs(z))) * LOG2E
    if causal is not None:
        sp = sp * causal[0]
    s = jnp.dot(sp.astype(BF16), tri, preferred_element_type=F32)
    x = z - s
    if causal is not None:
        x = x + causal[1]
    a = jnp.exp2(x)
    pv = jnp.dot(a.astype(BF16), vblk, preferred_element_type=F32)
    tot = jnp.sum(sp, axis=1, keepdims=True)
    return jnp.where(lo, tot[:bq], tot[bq:]), jnp.where(lo, pv[:bq], pv[bq:])


def _attn_kernel(q_ref, k_ref, v_ref, tri_ref, keep_ref, bias_ref, o_ref,
                 qs_scr, acc_scr, decay_scr, *, q_off):
    bq, bk = q_ref.shape[1], ATTN_BLOCK
    j_diag = (q_off + pl.program_id(2) * bq) // bk
    lo = lax.broadcasted_iota(jnp.int32, (1, LANES), 1) < HEAD_DIM
    q = q_ref[0]
    qs_scr[:bq] = jnp.where(lo, q, jnp.zeros_like(q))
    qs_scr[bq:] = jnp.where(lo, jnp.zeros_like(q), q)

    def block(jb, causal):
        start = pl.multiple_of(jb * bk, bk)
        return _attn_block(qs_scr[...], k_ref[0, pl.ds(start, bk), :],
                           v_ref[0, pl.ds(start, bk), :], tri_ref[...], causal)

    def diag_block():
        return block(j_diag, (keep_ref[...], bias_ref[...]))

    def any_live(decay):
        m = jnp.min(jnp.min(decay, axis=0, keepdims=True), axis=1, keepdims=True)
        return m[0, 0] < DEAD_LOG2_DECAY

    @pl.when(j_diag == 0)
    def _():
        _, pv = diag_block()
        o_ref[0] = pv.astype(o_ref.dtype)

    @pl.when(j_diag > 0)
    def _():
        tot0, pv0 = diag_block()
        tot1, pv1 = block(j_diag - 1, None)
        acc_scr[...] = pv0 + jnp.exp2(-tot0) * pv1
        decay = tot0 + tot1
        decay_scr[...] = decay

        def older(state):
            jb, _ = state
            tot, pv = block(jb, None)
            decay = decay_scr[...]
            acc_scr[...] += jnp.exp2(-decay) * pv
            decay = decay + tot
            decay_scr[...] = decay
            return jb - 1, any_live(decay)

        lax.while_loop(lambda state: jnp.logical_and(state[0] >= 0, state[1]), older,
                       (j_diag - 2, any_live(decay)))
        o_ref[0] = acc_scr[...].astype(o_ref.dtype)


def _attention(q, k, v, q_off):
    b, tq, _ = q.shape
    tk = k.shape[1]
    bq = min(tq, ATTN_BLOCK)
    assert bq & (bq - 1) == 0 and tq % bq == 0 and tk % ATTN_BLOCK == 0
    assert q_off % ATTN_BLOCK == 0 and (bq == ATTN_BLOCK or tq == bq)
    assert q_off + tq <= tk
    r = lax.broadcasted_iota(jnp.int32, (ATTN_BLOCK, ATTN_BLOCK), 0)
    c = lax.broadcasted_iota(jnp.int32, (ATTN_BLOCK, ATTN_BLOCK), 1)
    tri = (r >= c).astype(BF16)
    visible = jnp.tile(c[:bq] < r[:bq], (2, 1))
    keep = visible.astype(F32)
    bias = jnp.where(visible, 0.0, CAUSAL_BIAS).astype(F32)
    return pl.pallas_call(
        functools.partial(_attn_kernel, q_off=q_off),
        grid=(b, D_MODEL // LANES, tq // bq),
        in_specs=[pl.BlockSpec((1, bq, LANES), lambda bi, hp, qi: (bi, qi, hp)),
                  pl.BlockSpec((1, tk, LANES), lambda bi, hp, qi: (bi, 0, hp)),
                  pl.BlockSpec((1, tk, LANES), lambda bi, hp, qi: (bi, 0, hp)),
                  _resident((ATTN_BLOCK, ATTN_BLOCK)),
                  _resident((2 * bq, ATTN_BLOCK)),
                  _resident((2 * bq, ATTN_BLOCK))],
        out_specs=pl.BlockSpec((1, bq, LANES), lambda bi, hp, qi: (bi, qi, hp)),
        out_shape=jax.ShapeDtypeStruct(q.shape, BF16),
        scratch_shapes=[pltpu.VMEM((2 * bq, LANES), BF16),
                        pltpu.VMEM((bq, LANES), F32),
                        pltpu.VMEM((bq, LANES), F32)],
        compiler_params=_params(3),
        name="attn",
    )(q, k, v, tri, keep, bias)


def _mix_kernel(oa_ref, ob_ref, ga_ref, gb_ref, x_ref, mod_ref, wa_ref, wb_ref, wo_ref, g_ref,
                x1_ref, h2_ref):
    ya = jnp.dot(oa_ref[...], wa_ref[...], preferred_element_type=F32)
    yb = jnp.dot(ob_ref[...], wb_ref[...], preferred_element_type=F32)
    merged = ga_ref[...].astype(F32) * ya + gb_ref[...].astype(F32) * yb
    y = jnp.dot(merged.astype(BF16), wo_ref[...], preferred_element_type=F32)
    x1 = x_ref[...] + mod_ref[0, 2:3, :] * y
    x1_ref[...] = x1
    h2_ref[...] = _rms_modulate(x1, g_ref[...], mod_ref[0, 3:4, :], mod_ref[0, 4:5, :]).astype(BF16)


def _mix(oa, ob, ga, gb, x2d, mod, wa, wb, wo, g2, seq_len, tm):
    m = x2d.shape[0]
    tiles_per_seq = seq_len // tm
    row = pl.BlockSpec((tm, D_MODEL), lambda i: (i, 0))
    full = _resident((D_MODEL, D_MODEL))
    return pl.pallas_call(
        _mix_kernel,
        grid=(m // tm,),
        in_specs=[row, row, row, row, row,
                  pl.BlockSpec((1, N_MOD, D_MODEL), lambda i: (i // tiles_per_seq, 0, 0)),
                  full, full, full, _resident((1, D_MODEL))],
        out_specs=[row, row],
        out_shape=[jax.ShapeDtypeStruct((m, D_MODEL), F32), jax.ShapeDtypeStruct((m, D_MODEL), BF16)],
        compiler_params=_params(1),
        name="mix",
    )(oa, ob, ga, gb, x2d, mod, wa, wb, wo, g2)


def _ffn_kernel(h_ref, x_ref, mod_ref, wu_ref, wf_ref, bf_ref, prev_ref, wd_ref, g_ref,
                y_ref, nf_ref, act_scr, hist_scr, *, tiles_per_seq):
    tm, d_ff = act_scr.shape

    @pl.when(pl.program_id(0) % tiles_per_seq == 0)
    def _():
        hist_scr[...] = prev_ref[0]

    for c in range(d_ff // MXU_WIDTH):
        sl = slice(c * MXU_WIDTH, (c + 1) * MXU_WIDTH)
        up = jnp.dot(h_ref[...], wu_ref[0, :, sl], preferred_element_type=F32)
        gate = jnp.dot(h_ref[...], wu_ref[1, :, sl], preferred_element_type=F32)
        up_c = _causal_conv(up, hist_scr[:, sl], wf_ref[:, sl], bf_ref[:, sl])
        act_scr[:, sl] = (up_c * _sigmoid(up_c) * gate).astype(BF16)
        tail = up[tm - 2:tm, :]
        hist_scr[:, sl] = tail
        nf_ref[0, :, sl] = tail

    y = jnp.dot(act_scr[...], wd_ref[...], preferred_element_type=F32)
    x2 = x_ref[...] + mod_ref[0, 5:6, :] * y
    y_ref[...] = x2 * lax.rsqrt(jnp.mean(x2 * x2, axis=-1, keepdims=True) + EPS) * g_ref[...]


def _ffn(h2, x1, mod, w_up2, w_fconv, b_fconv, prev, w_down, g_final, seq_len, tm):
    m = x1.shape[0]
    d_ff = w_down.shape[0]
    assert d_ff % MXU_WIDTH == 0
    tiles_per_seq = seq_len // tm
    row = pl.BlockSpec((tm, D_MODEL), lambda i: (i, 0))
    state = (1, CONV_WIDTH - 1, d_ff)
    return pl.pallas_call(
        functools.partial(_ffn_kernel, tiles_per_seq=tiles_per_seq),
        grid=(m // tm,),
        in_specs=[row, row,
                  pl.BlockSpec((1, N_MOD, D_MODEL), lambda i: (i // tiles_per_seq, 0, 0)),
                  _resident((2, D_MODEL, d_ff)),
                  _resident((CONV_WIDTH, d_ff)),
                  _resident((1, d_ff)),
                  pl.BlockSpec(state, lambda i: (i // tiles_per_seq, 0, 0)),
                  _resident((d_ff, D_MODEL)),
                  _resident((1, D_MODEL))],
        out_specs=[row, pl.BlockSpec(state, lambda i: (i, 0, 0))],
        out_shape=[jax.ShapeDtypeStruct((m, D_MODEL), F32),
                   jax.ShapeDtypeStruct((m // tm, CONV_WIDTH - 1, d_ff), F32)],
        scratch_shapes=[pltpu.VMEM((tm, d_ff), BF16),
                        pltpu.VMEM((CONV_WIDTH - 1, d_ff), F32)],
        compiler_params=_params(1),
        name="ffn",
    )(h2, x1, mod, w_up2, w_fconv, b_fconv, prev, w_down, g_final)


def _trunk(x, mod, past_k, past_v, conv_prev, ffn_prev, w, tm):
    bsz, t, _ = x.shape
    x2d = x.reshape(bsz * t, D_MODEL)
    q, kf, vf, kb, vb, ob, ga, gb, new_conv = _inproj(
        x2d, mod, w["g1"], w["w_seg"], w["w_conv"], w["b_conv"], conv_prev, t, tm)
    shape3 = (bsz, t, D_MODEL)
    if past_k is None:
        k_all, v_all, q_off = kb.reshape(shape3), vb.reshape(shape3), 0
    else:
        q_off = past_k.shape[1]
        pad = -(q_off + t) % ATTN_BLOCK
        cat = lambda past, new: jnp.pad(
            jnp.concatenate([past.reshape(bsz, q_off, D_MODEL).astype(BF16), new.reshape(shape3)], axis=1),
            ((0, 0), (0, pad), (0, 0)))
        k_all, v_all = cat(past_k, kb), cat(past_v, vb)
    oa = _attention(q.reshape(shape3), k_all, v_all, q_off).reshape(bsz * t, D_MODEL)
    x1, h2 = _mix(oa, ob, ga, gb, x2d, mod, w["wa"], w["wb"], w["wo"], w["g2"], t, tm)
    y, new_ffn = _ffn(h2, x1, mod, w["w_up2"], w["w_fconv"], w["b_fconv"], ffn_prev, w["w_down"],
                      w["g_final"], t, tm)
    heads = (1, bsz, t, N_HEADS, HEAD_DIM)
    last = lambda s: s.reshape(bsz, t // tm, CONV_WIDTH - 1, -1)[None, :, -1]
    return (y.reshape(bsz, t, D_MODEL), kf.reshape(heads), vf.reshape(heads),
            last(new_conv), last(new_ffn))


def kernel(x_prompt, x_sample, cache_k, cache_v, state_conv, state_ffn_conv, c_prompt, c_sample,
           w_ada, b_ada, g_norm1, w_in, w_conv, b_conv, w_branch_a, w_branch_b, w_out,
           g_norm2, w_up, w_fconv, b_fconv, w_down, g_final):
    assert w_ada.shape[0] == 1, "one trunk layer"
    bp, bs = x_prompt.shape[0], x_sample.shape[0]
    d_ff = w_down.shape[1]
    w = {
        "g1": g_norm1[0].reshape(1, D_MODEL),
        "w_seg": w_in[0].astype(BF16).reshape(D_MODEL, N_SEG, D_MODEL).transpose(1, 0, 2),
        "w_conv": w_conv[0], "b_conv": b_conv[0].reshape(1, -1),
        "wa": w_branch_a[0].astype(BF16), "wb": w_branch_b[0].astype(BF16),
        "wo": w_out[0].astype(BF16),
        "g2": g_norm2[0].reshape(1, D_MODEL),
        "w_up2": w_up[0].astype(BF16).reshape(D_MODEL, 2, d_ff).transpose(1, 0, 2),
        "w_fconv": w_fconv[0], "b_fconv": b_fconv[0].reshape(1, -1),
        "w_down": w_down[0].astype(BF16),
        "g_final": g_final.reshape(1, D_MODEL),
    }
    mod = _ada(jnp.concatenate([c_prompt, c_sample], axis=0), w_ada[0], b_ada[0])
    mod = mod.reshape(bp + bs, N_MOD, D_MODEL)
    zeros = lambda width: jnp.zeros((bp, CONV_WIDTH - 1, width), x_prompt.dtype)
    yp, kp, vp, cp, fp = _trunk(x_prompt, mod[:bp], None, None, zeros(D_MODEL), zeros(d_ff), w, 512)
    ys, ks, vs, cs, fs = _trunk(x_sample, mod[bp:], cache_k[0], cache_v[0], state_conv[0],
                                state_ffn_conv[0], w, x_sample.shape[1])
    return (yp, ys, kp, vp, cp, fp, ks, vs, cs, fs)
```

```python
import functools
import math

import jax
import jax.numpy as jnp
from jax import lax
from jax.experimental import pallas as pl
from jax.experimental.pallas import tpu as pltpu

F32 = jnp.float32
BF16 = jnp.bfloat16

D_MODEL = 1024
N_HEADS = 16
HEAD_DIM = 64
N_SEG = 8
N_MOD = 6
CONV_WIDTH = 3
EPS = 1e-6
LANES = 128
SUBLANES = 8
MXU_WIDTH = 256
ATTN_BLOCK = 256
ATTN_LANE_TILES = 4
VMEM_LIMIT = 56 * 1024 * 1024
LOG2E = math.log2(math.e)
Q_SCALE = HEAD_DIM ** -0.5 * LOG2E
DEAD_LOG2_DECAY = 105.0 * LOG2E
CAUSAL_BIAS = -1e30


def _params(n_axes):
    return pltpu.CompilerParams(dimension_semantics=("arbitrary",) * n_axes,
                                vmem_limit_bytes=VMEM_LIMIT)


def _resident(shape):
    return pl.BlockSpec(shape, lambda *_: (0,) * len(shape), pipeline_mode=pl.Buffered(1))


def _sigmoid(x):
    return 1.0 / (1.0 + jnp.exp(-x))


def _rms_modulate(x, gain, shift, scale):
    y = x * lax.rsqrt(jnp.mean(x * x, axis=-1, keepdims=True) + EPS) * gain
    return y * (1.0 + scale) + shift


def _shift_rows(u, fill):
    r = pltpu.roll(u, 1, 0)
    is_row0 = lax.broadcasted_iota(jnp.int32, (SUBLANES, u.shape[1]), 0) == 0
    top = jnp.where(is_row0, fill, r[:SUBLANES])
    return jnp.concatenate([top, r[SUBLANES:]], axis=0)


def _causal_conv(u, hist, w, b):
    u1 = _shift_rows(u, hist[1:2])
    u2 = _shift_rows(u1, hist[0:1])
    return b + w[0:1] * u2 + w[1:2] * u1 + w[2:3] * u


def _ada_kernel(c_ref, w_ref, b_ref, o_ref):
    c = c_ref[...]
    o_ref[...] = jnp.dot(c * _sigmoid(c), w_ref[...], preferred_element_type=F32) + b_ref[...]


def _ada(c, w_ada, b_ada):
    n, tn = c.shape[0], 1536
    return pl.pallas_call(
        _ada_kernel,
        grid=(N_MOD * D_MODEL // tn,),
        in_specs=[pl.BlockSpec((n, D_MODEL), lambda j: (0, 0)),
                  pl.BlockSpec((D_MODEL, tn), lambda j: (0, j)),
                  pl.BlockSpec((1, tn), lambda j: (0, j))],
        out_specs=pl.BlockSpec((n, tn), lambda j: (0, j)),
        out_shape=jax.ShapeDtypeStruct((n, N_MOD * D_MODEL), F32),
        compiler_params=_params(1),
        name="ada",
    )(c, w_ada, b_ada.reshape(1, -1))


def _inproj_kernel(x_ref, mod_ref, g_ref, wt_ref, w_ref, wc_ref, bc_ref, prev_ref,
                   q_ref, kf_ref, vf_ref, kb_ref, vb_ref, ob_ref, ga_ref, gb_ref, nc_ref,
                   h_scr, hist_scr, *, tiles_per_seq):
    tm = x_ref.shape[0]

    @pl.when(pl.program_id(0) % tiles_per_seq == 0)
    def _():
        hist_scr[...] = prev_ref[0]

    h = _rms_modulate(x_ref[...], g_ref[...], mod_ref[0, 0:1, :], mod_ref[0, 1:2, :])
    h_scr[...] = h.astype(BF16)

    for s, f_ref, b_ref in ((0, kf_ref, kb_ref), (1, vf_ref, vb_ref)):
        t = lax.dot_general(wt_ref[s], h_scr[...], (((1,), (1,)), ((), ())),
                            preferred_element_type=F32)
        f_ref[0] = t
        b_ref[0] = t.astype(BF16)

    for c in range(D_MODEL // MXU_WIDTH):
        sl = slice(c * MXU_WIDTH, (c + 1) * MXU_WIDTH)

        def proj(s):
            return jnp.dot(h_scr[...], w_ref[s, :, sl], preferred_element_type=F32)

        q_ref[:, sl] = (proj(0) * Q_SCALE).astype(BF16)
        ga_ref[:, sl] = _sigmoid(proj(4)).astype(BF16)
        gb_ref[:, sl] = _sigmoid(proj(5)).astype(BF16)
        cu = proj(2) * proj(3)
        conv = _causal_conv(cu, hist_scr[:, sl], wc_ref[:, sl], bc_ref[:, sl])
        ob_ref[:, sl] = (proj(1) * conv).astype(BF16)
        tail = cu[tm - 2:tm, :]
        hist_scr[:, sl] = tail
        nc_ref[0, :, sl] = tail


def _inproj(x2d, mod, g1, w_kvt, w_seg, w_conv, b_conv, prev, seq_len, tm):
    m = x2d.shape[0]
    n_seq = m // seq_len
    tiles_per_seq = seq_len // tm
    row = pl.BlockSpec((tm, D_MODEL), lambda i: (i, 0))
    chan = pl.BlockSpec((1, D_MODEL, tm), lambda i: (i // tiles_per_seq, 0, i % tiles_per_seq))
    act = lambda dt: jax.ShapeDtypeStruct((m, D_MODEL), dt)
    act_t = lambda dt: jax.ShapeDtypeStruct((n_seq, D_MODEL, seq_len), dt)
    state = (1, CONV_WIDTH - 1, D_MODEL)
    return pl.pallas_call(
        functools.partial(_inproj_kernel, tiles_per_seq=tiles_per_seq),
        grid=(m // tm,),
        in_specs=[row,
                  pl.BlockSpec((1, N_MOD, D_MODEL), lambda i: (i // tiles_per_seq, 0, 0)),
                  _resident((1, D_MODEL)),
                  _resident((2, D_MODEL, D_MODEL)),
                  _resident((N_SEG - 2, D_MODEL, D_MODEL)),
                  _resident((CONV_WIDTH, D_MODEL)),
                  _resident((1, D_MODEL)),
                  pl.BlockSpec(state, lambda i: (i // tiles_per_seq, 0, 0))],
        out_specs=[row, chan, chan, chan, chan, row, row, row,
                   pl.BlockSpec(state, lambda i: (i, 0, 0))],
        out_shape=[act(BF16), act_t(F32), act_t(F32), act_t(BF16), act_t(BF16), act(BF16),
                   act(BF16), act(BF16),
                   jax.ShapeDtypeStruct((m // tm, CONV_WIDTH - 1, D_MODEL), F32)],
        scratch_shapes=[pltpu.VMEM((tm, D_MODEL), BF16),
                        pltpu.VMEM((CONV_WIDTH - 1, D_MODEL), F32)],
        compiler_params=_params(1),
        name="inproj",
    )(x2d, mod, g1, w_kvt, w_seg, w_conv, b_conv, prev)


def _attn_block(qs, kblk, vblk, tri, causal):
    bq = qs.shape[0] // 2
    lo = lax.broadcasted_iota(jnp.int32, (1, LANES), 1) < HEAD_DIM
    z = jnp.dot(qs, kblk, preferred_element_type=F32)
    sp = jnp.maximum(z, 0.0) + jnp.log(1.0 + jnp.exp2(-jnp.abs(z))) * LOG2E
    if causal is not None:
        sp = sp * causal[0]
    s = jnp.dot(sp.astype(BF16), tri, preferred_element_type=F32)
    x = z - s
    if causal is not None:
        x = x + causal[1]
    a = jnp.exp2(x)
    pv = lax.dot_general(a.astype(BF16), vblk, (((1,), (1,)), ((), ())),
                         preferred_element_type=F32)
    tot = jnp.sum(sp, axis=1, keepdims=True)
    return jnp.where(lo, tot[:bq], tot[bq:]), jnp.where(lo, pv[:bq], pv[bq:])


def _attn_kernel(q_ref, k_ref, v_ref, tri_ref, keep_ref, bias_ref, o_ref,
                 qs_scr, acc_scr, decay_scr, *, q_off):
    bq, bk = q_ref.shape[1], ATTN_BLOCK
    n_lt = q_ref.shape[2] // LANES
    j_diag = (q_off + pl.program_id(2) * bq) // bk
    lo = lax.broadcasted_iota(jnp.int32, (1, LANES), 1) < HEAD_DIM
    lanes = [slice(lt * LANES, (lt + 1) * LANES) for lt in range(n_lt)]
    for lt in range(n_lt):
        q = q_ref[0, :, lanes[lt]]
        qs_scr[lt, :bq] = jnp.where(lo, q, jnp.zeros_like(q))
        qs_scr[lt, bq:] = jnp.where(lo, jnp.zeros_like(q), q)

    def block(lt, jb, causal):
        start = pl.multiple_of(jb * bk, bk)
        return _attn_block(qs_scr[lt], k_ref[0, lanes[lt], pl.ds(start, bk)],
                           v_ref[0, lanes[lt], pl.ds(start, bk)], tri_ref[...], causal)

    def diag_block(lt):
        return block(lt, j_diag, (keep_ref[...], bias_ref[...]))

    def any_live(decays):
        m = functools.reduce(jnp.minimum, decays)
        m = jnp.min(jnp.min(m, axis=0, keepdims=True), axis=1, keepdims=True)
        return m[0, 0] < DEAD_LOG2_DECAY

    @pl.when(j_diag == 0)
    def _():
        for lt in range(n_lt):
            _, pv = diag_block(lt)
            o_ref[0, :, lanes[lt]] = pv.astype(o_ref.dtype)

    @pl.when(j_diag > 0)
    def _():
        newest = [(diag_block(lt), block(lt, j_diag - 1, None)) for lt in range(n_lt)]
        decays = []
        for lt, ((tot0, pv0), (tot1, pv1)) in enumerate(newest):
            acc_scr[lt] = pv0 + jnp.exp2(-tot0) * pv1
            decays.append(tot0 + tot1)
            decay_scr[lt] = decays[lt]

        def older(state):
            jb, _ = state
            blocks = [block(lt, jb, None) for lt in range(n_lt)]
            decays = []
            for lt, (tot, pv) in enumerate(blocks):
                decay = decay_scr[lt]
                acc_scr[lt] += jnp.exp2(-decay) * pv
                decays.append(decay + tot)
                decay_scr[lt] = decays[lt]
            return jb - 1, any_live(decays)

        lax.while_loop(lambda state: jnp.logical_and(state[0] >= 0, state[1]), older,
                       (j_diag - 2, any_live(decays)))
        for lt in range(n_lt):
            o_ref[0, :, lanes[lt]] = acc_scr[lt].astype(o_ref.dtype)


def _attention(q, kt, vt, q_off):
    b, tq, _ = q.shape
    tk = kt.shape[2]
    width = ATTN_LANE_TILES * LANES
    bq = min(tq, ATTN_BLOCK)
    assert bq & (bq - 1) == 0 and tq % bq == 0 and tk % ATTN_BLOCK == 0
    assert q_off % ATTN_BLOCK == 0 and (bq == ATTN_BLOCK or tq == bq)
    assert q_off + tq <= tk
    r = lax.broadcasted_iota(jnp.int32, (ATTN_BLOCK, ATTN_BLOCK), 0)
    c = lax.broadcasted_iota(jnp.int32, (ATTN_BLOCK, ATTN_BLOCK), 1)
    tri = (r >= c).astype(BF16)
    visible = jnp.tile(c[:bq] < r[:bq], (2, 1))
    keep = visible.astype(F32)
    bias = jnp.where(visible, 0.0, CAUSAL_BIAS).astype(F32)
    return pl.pallas_call(
        functools.partial(_attn_kernel, q_off=q_off),
        grid=(b, D_MODEL // width, tq // bq),
        in_specs=[pl.BlockSpec((1, bq, width), lambda bi, hg, qi: (bi, qi, hg)),
                  pl.BlockSpec((1, width, tk), lambda bi, hg, qi: (bi, hg, 0)),
                  pl.BlockSpec((1, width, tk), lambda bi, hg, qi: (bi, hg, 0)),
                  _resident((ATTN_BLOCK, ATTN_BLOCK)),
                  _resident((2 * bq, ATTN_BLOCK)),
                  _resident((2 * bq, ATTN_BLOCK))],
        out_specs=pl.BlockSpec((1, bq, width), lambda bi, hg, qi: (bi, qi, hg)),
        out_shape=jax.ShapeDtypeStruct(q.shape, BF16),
        scratch_shapes=[pltpu.VMEM((ATTN_LANE_TILES, 2 * bq, LANES), BF16),
                        pltpu.VMEM((ATTN_LANE_TILES, bq, LANES), F32),
                        pltpu.VMEM((ATTN_LANE_TILES, bq, LANES), F32)],
        compiler_params=_params(3),
        name="attn",
    )(q, kt, vt, tri, keep, bias)


def _mix_kernel(oa_ref, ob_ref, ga_ref, gb_ref, x_ref, mod_ref, wa_ref, wb_ref, wo_ref, g_ref,
                x1_ref, h2_ref):
    ya = jnp.dot(oa_ref[...], wa_ref[...], preferred_element_type=F32)
    yb = jnp.dot(ob_ref[...], wb_ref[...], preferred_element_type=F32)
    merged = ga_ref[...].astype(F32) * ya + gb_ref[...].astype(F32) * yb
    y = jnp.dot(merged.astype(BF16), wo_ref[...], preferred_element_type=F32)
    x1 = x_ref[...] + mod_ref[0, 2:3, :] * y
    x1_ref[...] = x1
    h2_ref[...] = _rms_modulate(x1, g_ref[...], mod_ref[0, 3:4, :], mod_ref[0, 4:5, :]).astype(BF16)


def _mix(oa, ob, ga, gb, x2d, mod, wa, wb, wo, g2, seq_len, tm):
    m = x2d.shape[0]
    tiles_per_seq = seq_len // tm
    row = pl.BlockSpec((tm, D_MODEL), lambda i: (i, 0))
    full = _resident((D_MODEL, D_MODEL))
    return pl.pallas_call(
        _mix_kernel,
        grid=(m // tm,),
        in_specs=[row, row, row, row, row,
                  pl.BlockSpec((1, N_MOD, D_MODEL), lambda i: (i // tiles_per_seq, 0, 0)),
                  full, full, full, _resident((1, D_MODEL))],
        out_specs=[row, row],
        out_shape=[jax.ShapeDtypeStruct((m, D_MODEL), F32), jax.ShapeDtypeStruct((m, D_MODEL), BF16)],
        compiler_params=_params(1),
        name="mix",
    )(oa, ob, ga, gb, x2d, mod, wa, wb, wo, g2)


def _ffn_kernel(h_ref, x_ref, mod_ref, wu_ref, wf_ref, bf_ref, prev_ref, wd_ref, g_ref,
                y_ref, nf_ref, act_scr, hist_scr, *, tiles_per_seq):
    tm, d_ff = act_scr.shape

    @pl.when(pl.program_id(0) % tiles_per_seq == 0)
    def _():
        hist_scr[...] = prev_ref[0]

    for c in range(d_ff // MXU_WIDTH):
        sl = slice(c * MXU_WIDTH, (c + 1) * MXU_WIDTH)
        up = jnp.dot(h_ref[...], wu_ref[0, :, sl], preferred_element_type=F32)
        gate = jnp.dot(h_ref[...], wu_ref[1, :, sl], preferred_element_type=F32)
        up_c = _causal_conv(up, hist_scr[:, sl], wf_ref[:, sl], bf_ref[:, sl])
        act_scr[:, sl] = (up_c * _sigmoid(up_c) * gate).astype(BF16)
        tail = up[tm - 2:tm, :]
        hist_scr[:, sl] = tail
        nf_ref[0, :, sl] = tail

    y = jnp.dot(act_scr[...], wd_ref[...], preferred_element_type=F32)
    x2 = x_ref[...] + mod_ref[0, 5:6, :] * y
    y_ref[...] = x2 * lax.rsqrt(jnp.mean(x2 * x2, axis=-1, keepdims=True) + EPS) * g_ref[...]


def _ffn(h2, x1, mod, w_up2, w_fconv, b_fconv, prev, w_down, g_final, seq_len, tm):
    m = x1.shape[0]
    d_ff = w_down.shape[0]
    assert d_ff % MXU_WIDTH == 0
    tiles_per_seq = seq_len // tm
    row = pl.BlockSpec((tm, D_MODEL), lambda i: (i, 0))
    state = (1, CONV_WIDTH - 1, d_ff)
    return pl.pallas_call(
        functools.partial(_ffn_kernel, tiles_per_seq=tiles_per_seq),
        grid=(m // tm,),
        in_specs=[row, row,
                  pl.BlockSpec((1, N_MOD, D_MODEL), lambda i: (i // tiles_per_seq, 0, 0)),
                  _resident((2, D_MODEL, d_ff)),
                  _resident((CONV_WIDTH, d_ff)),
                  _resident((1, d_ff)),
                  pl.BlockSpec(state, lambda i: (i // tiles_per_seq, 0, 0)),
                  _resident((d_ff, D_MODEL)),
                  _resident((1, D_MODEL))],
        out_specs=[row, pl.BlockSpec(state, lambda i: (i, 0, 0))],
        out_shape=[jax.ShapeDtypeStruct((m, D_MODEL), F32),
                   jax.ShapeDtypeStruct((m // tm, CONV_WIDTH - 1, d_ff), F32)],
        scratch_shapes=[pltpu.VMEM((tm, d_ff), BF16),
                        pltpu.VMEM((CONV_WIDTH - 1, d_ff), F32)],
        compiler_params=_params(1),
        name="ffn",
    )(h2, x1, mod, w_up2, w_fconv, b_fconv, prev, w_down, g_final)


def _trunk(x, mod, past_k, past_v, conv_prev, ffn_prev, w, tm):
    bsz, t, _ = x.shape
    x2d = x.reshape(bsz * t, D_MODEL)
    q, kf, vf, kb, vb, ob, ga, gb, new_conv = _inproj(
        x2d, mod, w["g1"], w["w_kvt"], w["w_seg"], w["w_conv"], w["b_conv"], conv_prev, t, tm)
    if past_k is None:
        k_all, v_all, q_off = kb, vb, 0
    else:
        q_off = past_k.shape[1]
        pad = -(q_off + t) % ATTN_BLOCK
        cat = lambda past, new: jnp.pad(
            jnp.concatenate([past.transpose(0, 2, 3, 1).reshape(bsz, D_MODEL, q_off).astype(BF16),
                             new], axis=2),
            ((0, 0), (0, 0), (0, pad)))
        k_all, v_all = cat(past_k, kb), cat(past_v, vb)
    oa = _attention(q.reshape(bsz, t, D_MODEL), k_all, v_all, q_off).reshape(bsz * t, D_MODEL)
    x1, h2 = _mix(oa, ob, ga, gb, x2d, mod, w["wa"], w["wb"], w["wo"], w["g2"], t, tm)
    y, new_ffn = _ffn(h2, x1, mod, w["w_up2"], w["w_fconv"], w["b_fconv"], ffn_prev, w["w_down"],
                      w["g_final"], t, tm)
    heads = lambda a: a.reshape(bsz, N_HEADS, HEAD_DIM, t).transpose(0, 3, 1, 2)[None]
    last = lambda s: s.reshape(bsz, t // tm, CONV_WIDTH - 1, -1)[None, :, -1]
    return (y.reshape(bsz, t, D_MODEL), heads(kf), heads(vf), last(new_conv), last(new_ffn))


def kernel(x_prompt, x_sample, cache_k, cache_v, state_conv, state_ffn_conv, c_prompt, c_sample,
           w_ada, b_ada, g_norm1, w_in, w_conv, b_conv, w_branch_a, w_branch_b, w_out,
           g_norm2, w_up, w_fconv, b_fconv, w_down, g_final):
    assert w_ada.shape[0] == 1, "one trunk layer"
    bp, bs = x_prompt.shape[0], x_sample.shape[0]
    d_ff = w_down.shape[1]
    w = {
        "g1": g_norm1[0].reshape(1, D_MODEL),
        "w_kvt": w_in[0][:, D_MODEL:3 * D_MODEL].astype(BF16).reshape(D_MODEL, 2, D_MODEL).transpose(1, 2, 0),
        "w_seg": jnp.concatenate([w_in[0][:, :D_MODEL], w_in[0][:, 3 * D_MODEL:]], axis=1)
        .astype(BF16).reshape(D_MODEL, N_SEG - 2, D_MODEL).transpose(1, 0, 2),
        "w_conv": w_conv[0], "b_conv": b_conv[0].reshape(1, -1),
        "wa": w_branch_a[0].astype(BF16), "wb": w_branch_b[0].astype(BF16),
        "wo": w_out[0].astype(BF16),
        "g2": g_norm2[0].reshape(1, D_MODEL),
        "w_up2": w_up[0].astype(BF16).reshape(D_MODEL, 2, d_ff).transpose(1, 0, 2),
        "w_fconv": w_fconv[0], "b_fconv": b_fconv[0].reshape(1, -1),
        "w_down": w_down[0].astype(BF16),
        "g_final": g_final.reshape(1, D_MODEL),
    }
    mod = _ada(jnp.concatenate([c_prompt, c_sample], axis=0), w_ada[0], b_ada[0])
    mod = mod.reshape(bp + bs, N_MOD, D_MODEL)
    zeros = lambda width: jnp.zeros((bp, CONV_WIDTH - 1, width), x_prompt.dtype)
    yp, kp, vp, cp, fp = _trunk(x_prompt, mod[:bp], None, None, zeros(D_MODEL), zeros(d_ff), w, 512)
    ys, ks, vs, cs, fs = _trunk(x_sample, mod[bp:], cache_k[0], cache_v[0], state_conv[0],
                                state_ffn_conv[0], w, x_sample.shape[1])
    return (yp, ys, kp, vp, cp, fp, ks, vs, cs, fs)
```

```python
import functools
import math

import jax
import jax.numpy as jnp
from jax import lax
from jax.experimental import pallas as pl
from jax.experimental.pallas import tpu as pltpu

F32 = jnp.float32
BF16 = jnp.bfloat16

D_MODEL = 1024
N_HEADS = 16
HEAD_DIM = 64
N_SEG = 8
SEG_Q, SEG_K, SEG_V, SEG_B, SEG_C, SEG_U, SEG_GA, SEG_GB = range(N_SEG)
N_MOD = 6
CONV_WIDTH = 3
EPS = 1e-6
LANES = 128
SUBLANES = 8
MXU_WIDTH = 256
ATTN_BLOCK = 256
ATTN_LANE_TILES = 8
VMEM_LIMIT = 56 * 1024 * 1024
LOG2E = math.log2(math.e)
Q_SCALE = HEAD_DIM ** -0.5 * LOG2E
DEAD_LOG2_DECAY = 105.0 * LOG2E
CAUSAL_BIAS = -1e30


def _params(n_axes):
    return pltpu.CompilerParams(dimension_semantics=("arbitrary",) * n_axes,
                                vmem_limit_bytes=VMEM_LIMIT)


def _resident(shape):
    return pl.BlockSpec(shape, lambda *_: (0,) * len(shape), pipeline_mode=pl.Buffered(1))


def _sigmoid(x):
    return 1.0 / (1.0 + jnp.exp(-x))


def _rms_modulate(x, gain, shift, scale):
    y = x * lax.rsqrt(jnp.mean(x * x, axis=-1, keepdims=True) + EPS) * gain
    return y * (1.0 + scale) + shift


def _shift_rows(u, fill):
    r = pltpu.roll(u, 1, 0)
    is_row0 = lax.broadcasted_iota(jnp.int32, (SUBLANES, u.shape[1]), 0) == 0
    top = jnp.where(is_row0, fill, r[:SUBLANES])
    return jnp.concatenate([top, r[SUBLANES:]], axis=0)


def _causal_conv(u, hist, w, b):
    u1 = _shift_rows(u, hist[1:2])
    u2 = _shift_rows(u1, hist[0:1])
    return b + w[0:1] * u2 + w[1:2] * u1 + w[2:3] * u


def _ada_kernel(c_ref, w_ref, b_ref, o_ref):
    c = c_ref[...]
    o_ref[...] = jnp.dot(c * _sigmoid(c), w_ref[...], preferred_element_type=F32) + b_ref[...]


def _ada(c, w_ada, b_ada):
    n, tn = c.shape[0], 1536
    return pl.pallas_call(
        _ada_kernel,
        grid=(N_MOD * D_MODEL // tn,),
        in_specs=[pl.BlockSpec((n, D_MODEL), lambda j: (0, 0)),
                  pl.BlockSpec((D_MODEL, tn), lambda j: (0, j)),
                  pl.BlockSpec((1, tn), lambda j: (0, j))],
        out_specs=pl.BlockSpec((n, tn), lambda j: (0, j)),
        out_shape=jax.ShapeDtypeStruct((n, N_MOD * D_MODEL), F32),
        compiler_params=_params(1),
        name="ada",
    )(c, w_ada, b_ada.reshape(1, -1))


def _inproj_kernel(x_ref, mod_ref, g_ref, wt_ref, w_ref, wc_ref, bc_ref, prev_ref,
                   q_ref, kf_ref, vf_ref, kb_ref, vb_ref, ob_ref, ga_ref, gb_ref, nc_ref,
                   h_scr, hist_scr, *, tiles_per_seq):
    tm = x_ref.shape[0]

    @pl.when(pl.program_id(0) % tiles_per_seq == 0)
    def _():
        hist_scr[...] = prev_ref[0]

    h = _rms_modulate(x_ref[...], g_ref[...], mod_ref[0, 0:1, :], mod_ref[0, 1:2, :])
    h_scr[...] = h.astype(BF16)

    for s, f_ref, b_ref in ((0, kf_ref, kb_ref), (1, vf_ref, vb_ref)):
        t = lax.dot_general(wt_ref[s], h_scr[...], (((1,), (1,)), ((), ())),
                            preferred_element_type=F32)
        f_ref[0] = t
        b_ref[0] = t.astype(BF16)

    for c in range(D_MODEL // MXU_WIDTH):
        sl = slice(c * MXU_WIDTH, (c + 1) * MXU_WIDTH)

        def proj(seg):
            cols = slice(seg * D_MODEL + sl.start, seg * D_MODEL + sl.stop)
            return jnp.dot(h_scr[...], w_ref[:, cols], preferred_element_type=F32)

        q_ref[:, sl] = (proj(SEG_Q) * Q_SCALE).astype(BF16)
        ga_ref[:, sl] = _sigmoid(proj(SEG_GA)).astype(BF16)
        gb_ref[:, sl] = _sigmoid(proj(SEG_GB)).astype(BF16)
        cu = proj(SEG_C) * proj(SEG_U)
        conv = _causal_conv(cu, hist_scr[:, sl], wc_ref[:, sl], bc_ref[:, sl])
        ob_ref[:, sl] = (proj(SEG_B) * conv).astype(BF16)
        tail = cu[tm - 2:tm, :]
        hist_scr[:, sl] = tail
        nc_ref[0, :, sl] = tail


def _inproj(x2d, mod, g1, w_kvt, w_seg, w_conv, b_conv, prev, seq_len, tm):
    m = x2d.shape[0]
    n_seq = m // seq_len
    tiles_per_seq = seq_len // tm
    row = pl.BlockSpec((tm, D_MODEL), lambda i: (i, 0))
    chan = pl.BlockSpec((1, D_MODEL, tm), lambda i: (i // tiles_per_seq, 0, i % tiles_per_seq))
    act = lambda dt: jax.ShapeDtypeStruct((m, D_MODEL), dt)
    act_t = lambda dt: jax.ShapeDtypeStruct((n_seq, D_MODEL, seq_len), dt)
    state = (1, CONV_WIDTH - 1, D_MODEL)
    return pl.pallas_call(
        functools.partial(_inproj_kernel, tiles_per_seq=tiles_per_seq),
        grid=(m // tm,),
        in_specs=[row,
                  pl.BlockSpec((1, N_MOD, D_MODEL), lambda i: (i // tiles_per_seq, 0, 0)),
                  _resident((1, D_MODEL)),
                  _resident((2, D_MODEL, D_MODEL)),
                  _resident((D_MODEL, N_SEG * D_MODEL)),
                  _resident((CONV_WIDTH, D_MODEL)),
                  _resident((1, D_MODEL)),
                  pl.BlockSpec(state, lambda i: (i // tiles_per_seq, 0, 0))],
        out_specs=[row, chan, chan, chan, chan, row, row, row,
                   pl.BlockSpec(state, lambda i: (i, 0, 0))],
        out_shape=[act(BF16), act_t(F32), act_t(F32), act_t(BF16), act_t(BF16), act(BF16),
                   act(BF16), act(BF16),
                   jax.ShapeDtypeStruct((m // tm, CONV_WIDTH - 1, D_MODEL), F32)],
        scratch_shapes=[pltpu.VMEM((tm, D_MODEL), BF16),
                        pltpu.VMEM((CONV_WIDTH - 1, D_MODEL), F32)],
        compiler_params=_params(1),
        name="inproj",
    )(x2d, mod, g1, w_kvt, w_seg, w_conv, b_conv, prev)


def _attn_block(qs, kblk, vblk, tri, causal):
    bq = qs.shape[0] // 2
    lo = lax.broadcasted_iota(jnp.int32, (1, LANES), 1) < HEAD_DIM
    z = jnp.dot(qs, kblk, preferred_element_type=F32)
    sp = jnp.maximum(z, 0.0) + jnp.log(1.0 + jnp.exp2(-jnp.abs(z))) * LOG2E
    if causal is not None:
        sp = sp * causal[0]
    s = jnp.dot(sp.astype(BF16), tri, preferred_element_type=F32)
    x = z - s
    if causal is not None:
        x = x + causal[1]
    a = jnp.exp2(x)
    pv = lax.dot_general(a.astype(BF16), vblk, (((1,), (1,)), ((), ())),
                         preferred_element_type=F32)
    tot = jnp.sum(sp, axis=1, keepdims=True)
    return jnp.where(lo, tot[:bq], tot[bq:]), jnp.where(lo, pv[:bq], pv[bq:])


def _attn_kernel(q_ref, k_ref, v_ref, tri_ref, keep_ref, bias_ref, o_ref,
                 qs_scr, acc_scr, decay_scr, *, q_off):
    bq, bk = q_ref.shape[1], ATTN_BLOCK
    n_lt = q_ref.shape[2] // LANES
    j_diag = (q_off + pl.program_id(2) * bq) // bk
    lo = lax.broadcasted_iota(jnp.int32, (1, LANES), 1) < HEAD_DIM
    lanes = [slice(lt * LANES, (lt + 1) * LANES) for lt in range(n_lt)]
    for lt in range(n_lt):
        q = q_ref[0, :, lanes[lt]]
        qs_scr[lt, :bq] = jnp.where(lo, q, jnp.zeros_like(q))
        qs_scr[lt, bq:] = jnp.where(lo, jnp.zeros_like(q), q)

    def block(lt, jb, causal):
        start = pl.multiple_of(jb * bk, bk)
        return _attn_block(qs_scr[lt], k_ref[0, lanes[lt], pl.ds(start, bk)],
                           v_ref[0, lanes[lt], pl.ds(start, bk)], tri_ref[...], causal)

    def diag_block(lt):
        return block(lt, j_diag, (keep_ref[...], bias_ref[...]))

    def any_live(decays):
        m = functools.reduce(jnp.minimum, decays)
        m = jnp.min(jnp.min(m, axis=0, keepdims=True), axis=1, keepdims=True)
        return m[0, 0] < DEAD_LOG2_DECAY

    @pl.when(j_diag == 0)
    def _():
        for lt in range(n_lt):
            _, pv = diag_block(lt)
            o_ref[0, :, lanes[lt]] = pv.astype(o_ref.dtype)

    @pl.when(j_diag > 0)
    def _():
        newest = [(diag_block(lt), block(lt, j_diag - 1, None)) for lt in range(n_lt)]
        decays = []
        for lt, ((tot0, pv0), (tot1, pv1)) in enumerate(newest):
            acc_scr[lt] = pv0 + jnp.exp2(-tot0) * pv1
            decays.append(tot0 + tot1)
            decay_scr[lt] = decays[lt]

        def older(state):
            jb, _ = state
            blocks = [block(lt, jb, None) for lt in range(n_lt)]
            decays = []
            for lt, (tot, pv) in enumerate(blocks):
                decay = decay_scr[lt]
                acc_scr[lt] += jnp.exp2(-decay) * pv
                decays.append(decay + tot)
                decay_scr[lt] = decays[lt]
            return jb - 1, any_live(decays)

        lax.while_loop(lambda state: jnp.logical_and(state[0] >= 0, state[1]), older,
                       (j_diag - 2, any_live(decays)))
        for lt in range(n_lt):
            o_ref[0, :, lanes[lt]] = acc_scr[lt].astype(o_ref.dtype)


def _attention(q, kt, vt, q_off):
    b, tq, _ = q.shape
    tk = kt.shape[2]
    width = ATTN_LANE_TILES * LANES
    bq = min(tq, ATTN_BLOCK)
    assert bq & (bq - 1) == 0 and tq % bq == 0 and tk % ATTN_BLOCK == 0
    assert q_off % ATTN_BLOCK == 0 and (bq == ATTN_BLOCK or tq == bq)
    assert q_off + tq <= tk
    r = lax.broadcasted_iota(jnp.int32, (ATTN_BLOCK, ATTN_BLOCK), 0)
    c = lax.broadcasted_iota(jnp.int32, (ATTN_BLOCK, ATTN_BLOCK), 1)
    tri = (r >= c).astype(BF16)
    visible = jnp.tile(c[:bq] < r[:bq], (2, 1))
    keep = visible.astype(F32)
    bias = jnp.where(visible, 0.0, CAUSAL_BIAS).astype(F32)
    return pl.pallas_call(
        functools.partial(_attn_kernel, q_off=q_off),
        grid=(b, D_MODEL // width, tq // bq),
        in_specs=[pl.BlockSpec((1, bq, width), lambda bi, hg, qi: (bi, qi, hg)),
                  pl.BlockSpec((1, width, tk), lambda bi, hg, qi: (bi, hg, 0)),
                  pl.BlockSpec((1, width, tk), lambda bi, hg, qi: (bi, hg, 0)),
                  _resident((ATTN_BLOCK, ATTN_BLOCK)),
                  _resident((2 * bq, ATTN_BLOCK)),
                  _resident((2 * bq, ATTN_BLOCK))],
        out_specs=pl.BlockSpec((1, bq, width), lambda bi, hg, qi: (bi, qi, hg)),
        out_shape=jax.ShapeDtypeStruct(q.shape, BF16),
        scratch_shapes=[pltpu.VMEM((ATTN_LANE_TILES, 2 * bq, LANES), BF16),
                        pltpu.VMEM((ATTN_LANE_TILES, bq, LANES), F32),
                        pltpu.VMEM((ATTN_LANE_TILES, bq, LANES), F32)],
        compiler_params=_params(3),
        name="attn",
    )(q, kt, vt, tri, keep, bias)


def _mix_kernel(oa_ref, ob_ref, ga_ref, gb_ref, x_ref, mod_ref, wa_ref, wb_ref, wo_ref, g_ref,
                x1_ref, h2_ref):
    ya = jnp.dot(oa_ref[...], wa_ref[...], preferred_element_type=F32)
    yb = jnp.dot(ob_ref[...], wb_ref[...], preferred_element_type=F32)
    merged = ga_ref[...].astype(F32) * ya + gb_ref[...].astype(F32) * yb
    y = jnp.dot(merged.astype(BF16), wo_ref[...], preferred_element_type=F32)
    x1 = x_ref[...] + mod_ref[0, 2:3, :] * y
    x1_ref[...] = x1
    h2_ref[...] = _rms_modulate(x1, g_ref[...], mod_ref[0, 3:4, :], mod_ref[0, 4:5, :]).astype(BF16)


def _mix(oa, ob, ga, gb, x2d, mod, wa, wb, wo, g2, seq_len, tm):
    m = x2d.shape[0]
    tiles_per_seq = seq_len // tm
    row = pl.BlockSpec((tm, D_MODEL), lambda i: (i, 0))
    full = _resident((D_MODEL, D_MODEL))
    return pl.pallas_call(
        _mix_kernel,
        grid=(m // tm,),
        in_specs=[row, row, row, row, row,
                  pl.BlockSpec((1, N_MOD, D_MODEL), lambda i: (i // tiles_per_seq, 0, 0)),
                  full, full, full, _resident((1, D_MODEL))],
        out_specs=[row, row],
        out_shape=[jax.ShapeDtypeStruct((m, D_MODEL), F32), jax.ShapeDtypeStruct((m, D_MODEL), BF16)],
        compiler_params=_params(1),
        name="mix",
    )(oa, ob, ga, gb, x2d, mod, wa, wb, wo, g2)


def _ffn_kernel(h_ref, x_ref, mod_ref, wu_ref, wf_ref, bf_ref, prev_ref, wd_ref, g_ref,
                y_ref, nf_ref, act_scr, hist_scr, *, tiles_per_seq):
    tm, d_ff = act_scr.shape

    @pl.when(pl.program_id(0) % tiles_per_seq == 0)
    def _():
        hist_scr[...] = prev_ref[0]

    for c in range(d_ff // MXU_WIDTH):
        sl = slice(c * MXU_WIDTH, (c + 1) * MXU_WIDTH)
        up = jnp.dot(h_ref[...], wu_ref[:, sl], preferred_element_type=F32)
        gate = jnp.dot(h_ref[...], wu_ref[:, slice(d_ff + sl.start, d_ff + sl.stop)],
                       preferred_element_type=F32)
        up_c = _causal_conv(up, hist_scr[:, sl], wf_ref[:, sl], bf_ref[:, sl])
        act_scr[:, sl] = (up_c * _sigmoid(up_c) * gate).astype(BF16)
        tail = up[tm - 2:tm, :]
        hist_scr[:, sl] = tail
        nf_ref[0, :, sl] = tail

    y = jnp.dot(act_scr[...], wd_ref[...], preferred_element_type=F32)
    x2 = x_ref[...] + mod_ref[0, 5:6, :] * y
    y_ref[...] = x2 * lax.rsqrt(jnp.mean(x2 * x2, axis=-1, keepdims=True) + EPS) * g_ref[...]


def _ffn(h2, x1, mod, w_up2, w_fconv, b_fconv, prev, w_down, g_final, seq_len, tm):
    m = x1.shape[0]
    d_ff = w_down.shape[0]
    assert d_ff % MXU_WIDTH == 0
    tiles_per_seq = seq_len // tm
    row = pl.BlockSpec((tm, D_MODEL), lambda i: (i, 0))
    state = (1, CONV_WIDTH - 1, d_ff)
    return pl.pallas_call(
        functools.partial(_ffn_kernel, tiles_per_seq=tiles_per_seq),
        grid=(m // tm,),
        in_specs=[row, row,
                  pl.BlockSpec((1, N_MOD, D_MODEL), lambda i: (i // tiles_per_seq, 0, 0)),
                  _resident((D_MODEL, 2 * d_ff)),
                  _resident((CONV_WIDTH, d_ff)),
                  _resident((1, d_ff)),
                  pl.BlockSpec(state, lambda i: (i // tiles_per_seq, 0, 0)),
                  _resident((d_ff, D_MODEL)),
                  _resident((1, D_MODEL))],
        out_specs=[row, pl.BlockSpec(state, lambda i: (i, 0, 0))],
        out_shape=[jax.ShapeDtypeStruct((m, D_MODEL), F32),
                   jax.ShapeDtypeStruct((m // tm, CONV_WIDTH - 1, d_ff), F32)],
        scratch_shapes=[pltpu.VMEM((tm, d_ff), BF16),
                        pltpu.VMEM((CONV_WIDTH - 1, d_ff), F32)],
        compiler_params=_params(1),
        name="ffn",
    )(h2, x1, mod, w_up2, w_fconv, b_fconv, prev, w_down, g_final)


def _trunk(x, mod, past_k, past_v, conv_prev, ffn_prev, w, tm):
    bsz, t, _ = x.shape
    x2d = x.reshape(bsz * t, D_MODEL)
    q, kf, vf, kb, vb, ob, ga, gb, new_conv = _inproj(
        x2d, mod, w["g1"], w["w_kvt"], w["w_seg"], w["w_conv"], w["b_conv"], conv_prev, t, tm)
    if past_k is None:
        k_all, v_all, q_off = kb, vb, 0
    else:
        q_off = past_k.shape[1]
        pad = -(q_off + t) % ATTN_BLOCK
        cat = lambda past, new: jnp.pad(
            jnp.concatenate([past.transpose(0, 2, 3, 1).reshape(bsz, D_MODEL, q_off).astype(BF16),
                             new], axis=2),
            ((0, 0), (0, 0), (0, pad)))
        k_all, v_all = cat(past_k, kb), cat(past_v, vb)
    oa = _attention(q.reshape(bsz, t, D_MODEL), k_all, v_all, q_off).reshape(bsz * t, D_MODEL)
    x1, h2 = _mix(oa, ob, ga, gb, x2d, mod, w["wa"], w["wb"], w["wo"], w["g2"], t, tm)
    y, new_ffn = _ffn(h2, x1, mod, w["w_up2"], w["w_fconv"], w["b_fconv"], ffn_prev, w["w_down"],
                      w["g_final"], t, tm)
    heads = lambda a: a.reshape(bsz, N_HEADS, HEAD_DIM, t).transpose(0, 3, 1, 2)[None]
    last = lambda s: s.reshape(bsz, t // tm, CONV_WIDTH - 1, -1)[None, :, -1]
    return (y.reshape(bsz, t, D_MODEL), heads(kf), heads(vf), last(new_conv), last(new_ffn))


def kernel(x_prompt, x_sample, cache_k, cache_v, state_conv, state_ffn_conv, c_prompt, c_sample,
           w_ada, b_ada, g_norm1, w_in, w_conv, b_conv, w_branch_a, w_branch_b, w_out,
           g_norm2, w_up, w_fconv, b_fconv, w_down, g_final):
    assert w_ada.shape[0] == 1, "one trunk layer"
    bp, bs = x_prompt.shape[0], x_sample.shape[0]
    d_ff = w_down.shape[1]
    w = {
        "g1": g_norm1[0].reshape(1, D_MODEL),
        "w_kvt": w_in[0][:, SEG_K * D_MODEL:(SEG_V + 1) * D_MODEL].astype(BF16)
        .reshape(D_MODEL, 2, D_MODEL).transpose(1, 2, 0),
        "w_seg": w_in[0].astype(BF16),
        "w_conv": w_conv[0], "b_conv": b_conv[0].reshape(1, -1),
        "wa": w_branch_a[0].astype(BF16), "wb": w_branch_b[0].astype(BF16),
        "wo": w_out[0].astype(BF16),
        "g2": g_norm2[0].reshape(1, D_MODEL),
        "w_up2": w_up[0].astype(BF16),
        "w_fconv": w_fconv[0], "b_fconv": b_fconv[0].reshape(1, -1),
        "w_down": w_down[0].astype(BF16),
        "g_final": g_final.reshape(1, D_MODEL),
    }
    mod = _ada(jnp.concatenate([c_prompt, c_sample], axis=0), w_ada[0], b_ada[0])
    mod = mod.reshape(bp + bs, N_MOD, D_MODEL)
    zeros = lambda width: jnp.zeros((bp, CONV_WIDTH - 1, width), x_prompt.dtype)
    yp, kp, vp, cp, fp = _trunk(x_prompt, mod[:bp], None, None, zeros(D_MODEL), zeros(d_ff), w, 512)
    ys, ks, vs, cs, fs = _trunk(x_sample, mod[bp:], cache_k[0], cache_v[0], state_conv[0],
                                state_ffn_conv[0], w, x_sample.shape[1])
    return (yp, ys, kp, vp, cp, fp, ks, vs, cs, fs)
```

```python
import functools
import math

import jax
import jax.numpy as jnp
from jax import lax
from jax.experimental import pallas as pl
from jax.experimental.pallas import tpu as pltpu

F32 = jnp.float32
BF16 = jnp.bfloat16

D_MODEL = 1024
N_HEADS = 16
HEAD_DIM = 64
N_SEG = 8
SEG_Q, SEG_K, SEG_V, SEG_B, SEG_C, SEG_U, SEG_GA, SEG_GB = range(N_SEG)
N_MOD = 6
CONV_WIDTH = 3
EPS = 1e-6
LANES = 128
SUBLANES = 8
MXU_WIDTH = 256
ATTN_BLOCK = 256
ATTN_LANE_TILES = 8
VMEM_LIMIT = 56 * 1024 * 1024
LOG2E = math.log2(math.e)
Q_SCALE = HEAD_DIM ** -0.5 * LOG2E
DEAD_LOG2_DECAY = 105.0 * LOG2E
CAUSAL_BIAS = -1e30


def _params(n_axes):
    return pltpu.CompilerParams(dimension_semantics=("arbitrary",) * n_axes,
                                vmem_limit_bytes=VMEM_LIMIT)


def _resident(shape):
    return pl.BlockSpec(shape, lambda *_: (0,) * len(shape), pipeline_mode=pl.Buffered(1))


def _sigmoid(x):
    return 1.0 / (1.0 + jnp.exp(-x))


def _rms_modulate(x, gain, shift, scale):
    y = x * lax.rsqrt(jnp.mean(x * x, axis=-1, keepdims=True) + EPS) * gain
    return y * (1.0 + scale) + shift


def _shift_rows(u, fill):
    r = pltpu.roll(u, 1, 0)
    is_row0 = lax.broadcasted_iota(jnp.int32, (SUBLANES, u.shape[1]), 0) == 0
    top = jnp.where(is_row0, fill, r[:SUBLANES])
    return jnp.concatenate([top, r[SUBLANES:]], axis=0)


def _causal_conv(u, hist, w, b):
    u1 = _shift_rows(u, hist[1:2])
    u2 = _shift_rows(u1, hist[0:1])
    return b + w[0:1] * u2 + w[1:2] * u1 + w[2:3] * u


def _conv_tile(u, sl, w_ref, b_ref, prev_ref, hist_scr, new_ref):
    n_seq = prev_ref.shape[0]
    rows = u.shape[0] // n_seq
    out = []
    for s in range(n_seq):
        part = u[s * rows:(s + 1) * rows]
        hist = hist_scr[:, sl] if n_seq == 1 else prev_ref[s, :, sl]
        out.append(_causal_conv(part, hist, w_ref[:, sl], b_ref[:, sl]))
        tail = part[rows - 2:rows]
        if n_seq == 1:
            hist_scr[:, sl] = tail
        new_ref[s, :, sl] = tail
    return out[0] if n_seq == 1 else jnp.concatenate(out, axis=0)


def _start_of_sequence(prev_ref, hist_scr, tiles_per_seq):
    if prev_ref.shape[0] == 1:
        @pl.when(pl.program_id(0) % tiles_per_seq == 0)
        def _():
            hist_scr[...] = prev_ref[0]


def _mod_rows(mod_ref, k, tm):
    n_seq = mod_ref.shape[0]
    if n_seq == 1:
        return mod_ref[0, k:k + 1, :]
    return jnp.concatenate([jnp.broadcast_to(mod_ref[s, k:k + 1, :], (tm // n_seq, D_MODEL))
                            for s in range(n_seq)], axis=0)


def _seq_tiling(seq_len, tm):
    assert tm % seq_len == 0 or seq_len % tm == 0
    return max(1, tm // seq_len), max(1, seq_len // tm)


def _ada_kernel(c_ref, w_ref, b_ref, o_ref):
    c = c_ref[...]
    o_ref[...] = jnp.dot(c * _sigmoid(c), w_ref[...], preferred_element_type=F32) + b_ref[...]


def _ada(c, w_ada, b_ada):
    n, tn = c.shape[0], 1536
    return pl.pallas_call(
        _ada_kernel,
        grid=(N_MOD * D_MODEL // tn,),
        in_specs=[pl.BlockSpec((n, D_MODEL), lambda j: (0, 0)),
                  pl.BlockSpec((D_MODEL, tn), lambda j: (0, j)),
                  pl.BlockSpec((1, tn), lambda j: (0, j))],
        out_specs=pl.BlockSpec((n, tn), lambda j: (0, j)),
        out_shape=jax.ShapeDtypeStruct((n, N_MOD * D_MODEL), F32),
        compiler_params=_params(1),
        name="ada",
    )(c, w_ada, b_ada.reshape(1, -1))


def _inproj_kernel(x_ref, mod_ref, g_ref, wt_ref, w_ref, wc_ref, bc_ref, prev_ref,
                   q_ref, kf_ref, vf_ref, kb_ref, vb_ref, ob_ref, ga_ref, gb_ref, nc_ref,
                   hist_scr, *, tiles_per_seq):
    tm = x_ref.shape[0]
    _start_of_sequence(prev_ref, hist_scr, tiles_per_seq)
    h = _rms_modulate(x_ref[...], g_ref[...], _mod_rows(mod_ref, 0, tm), _mod_rows(mod_ref, 1, tm))
    hb = h.astype(BF16)

    for c in range(D_MODEL // MXU_WIDTH):
        sl = slice(c * MXU_WIDTH, (c + 1) * MXU_WIDTH)

        def proj(seg):
            cols = slice(seg * D_MODEL + sl.start, seg * D_MODEL + sl.stop)
            return jnp.dot(hb, w_ref[:, cols], preferred_element_type=F32)

        q_ref[:, sl] = (proj(SEG_Q) * Q_SCALE).astype(BF16)
        ga_ref[:, sl] = _sigmoid(proj(SEG_GA)).astype(BF16)
        gb_ref[:, sl] = _sigmoid(proj(SEG_GB)).astype(BF16)
        cu = proj(SEG_C) * proj(SEG_U)
        conv = _conv_tile(cu, sl, wc_ref, bc_ref, prev_ref, hist_scr, nc_ref)
        ob_ref[:, sl] = (proj(SEG_B) * conv).astype(BF16)

    n_seq = kf_ref.shape[0]
    for s, f_ref, b_ref in ((0, kf_ref, kb_ref), (1, vf_ref, vb_ref)):
        t = lax.dot_general(wt_ref[s], hb, (((1,), (1,)), ((), ())),
                            preferred_element_type=F32)
        for sq in range(n_seq):
            cols = slice(sq * (tm // n_seq), (sq + 1) * (tm // n_seq))
            f_ref[sq] = t[:, cols]
            b_ref[sq] = t[:, cols].astype(BF16)


def _inproj(x2d, mod, g1, w_kvt, w_seg, w_conv, b_conv, prev, seq_len, tm):
    m = x2d.shape[0]
    n_seq = m // seq_len
    spt, tiles_per_seq = _seq_tiling(seq_len, tm)
    row = pl.BlockSpec((tm, D_MODEL), lambda i: (i, 0))
    chan = pl.BlockSpec((spt, D_MODEL, tm // spt),
                        lambda i: (i // tiles_per_seq, 0, i % tiles_per_seq))
    act = lambda dt: jax.ShapeDtypeStruct((m, D_MODEL), dt)
    act_t = lambda dt: jax.ShapeDtypeStruct((n_seq, D_MODEL, seq_len), dt)
    state = (spt, CONV_WIDTH - 1, D_MODEL)
    return pl.pallas_call(
        functools.partial(_inproj_kernel, tiles_per_seq=tiles_per_seq),
        grid=(m // tm,),
        in_specs=[row,
                  pl.BlockSpec((spt, N_MOD, D_MODEL), lambda i: (i // tiles_per_seq, 0, 0)),
                  _resident((1, D_MODEL)),
                  _resident((2, D_MODEL, D_MODEL)),
                  _resident((D_MODEL, N_SEG * D_MODEL)),
                  _resident((CONV_WIDTH, D_MODEL)),
                  _resident((1, D_MODEL)),
                  pl.BlockSpec(state, lambda i: (i // tiles_per_seq, 0, 0))],
        out_specs=[row, chan, chan, chan, chan, row, row, row,
                   pl.BlockSpec(state, lambda i: (i, 0, 0))],
        out_shape=[act(BF16), act_t(F32), act_t(F32), act_t(BF16), act_t(BF16), act(BF16),
                   act(BF16), act(BF16),
                   jax.ShapeDtypeStruct((m // tm * spt, CONV_WIDTH - 1, D_MODEL), F32)],
        scratch_shapes=[pltpu.VMEM((CONV_WIDTH - 1, D_MODEL), F32)],
        compiler_params=_params(1),
        name="inproj",
    )(x2d, mod, g1, w_kvt, w_seg, w_conv, b_conv, prev)


def _attn_block(qs, kblk, vblk, tri, causal):
    bq = qs.shape[0] // 2
    lo = lax.broadcasted_iota(jnp.int32, (1, LANES), 1) < HEAD_DIM
    z = jnp.dot(qs, kblk, preferred_element_type=F32)
    sp = jnp.maximum(z, 0.0) + jnp.log(1.0 + jnp.exp2(-jnp.abs(z))) * LOG2E
    if causal is not None:
        sp = sp * causal[0]
    s = jnp.dot(sp.astype(BF16), tri, preferred_element_type=F32)
    x = z - s
    if causal is not None:
        x = x + causal[1]
    a = jnp.exp2(x)
    pv = lax.dot_general(a.astype(BF16), vblk, (((1,), (1,)), ((), ())),
                         preferred_element_type=F32)
    tot = jnp.sum(sp, axis=1, keepdims=True)
    return jnp.where(lo, tot[:bq], tot[bq:]), jnp.where(lo, pv[:bq], pv[bq:])


def _attn_kernel(q_ref, k_ref, v_ref, tri_ref, keep_ref, bias_ref, o_ref,
                 qs_scr, acc_scr, decay_scr, *, q_off):
    bq, bk = q_ref.shape[1], ATTN_BLOCK
    n_lt = q_ref.shape[2] // LANES
    j_diag = (q_off + pl.program_id(2) * bq) // bk
    lo = lax.broadcasted_iota(jnp.int32, (1, LANES), 1) < HEAD_DIM
    lanes = [slice(lt * LANES, (lt + 1) * LANES) for lt in range(n_lt)]
    for lt in range(n_lt):
        q = q_ref[0, :, lanes[lt]]
        qs_scr[lt, :bq] = jnp.where(lo, q, jnp.zeros_like(q))
        qs_scr[lt, bq:] = jnp.where(lo, jnp.zeros_like(q), q)

    def block(lt, jb, causal):
        start = pl.multiple_of(jb * bk, bk)
        return _attn_block(qs_scr[lt], k_ref[0, lanes[lt], pl.ds(start, bk)],
                           v_ref[0, lanes[lt], pl.ds(start, bk)], tri_ref[...], causal)

    def diag_block(lt):
        return block(lt, j_diag, (keep_ref[...], bias_ref[...]))

    def any_live(decays):
        m = functools.reduce(jnp.minimum, decays)
        m = jnp.min(jnp.min(m, axis=0, keepdims=True), axis=1, keepdims=True)
        return m[0, 0] < DEAD_LOG2_DECAY

    @pl.when(j_diag == 0)
    def _():
        for lt in range(n_lt):
            _, pv = diag_block(lt)
            o_ref[0, :, lanes[lt]] = pv.astype(o_ref.dtype)

    @pl.when(j_diag > 0)
    def _():
        newest = [(diag_block(lt), block(lt, j_diag - 1, None)) for lt in range(n_lt)]
        decays = []
        for lt, ((tot0, pv0), (tot1, pv1)) in enumerate(newest):
            acc_scr[lt] = pv0 + jnp.exp2(-tot0) * pv1
            decays.append(tot0 + tot1)
            decay_scr[lt] = decays[lt]

        def older(state):
            jb, _ = state
            blocks = [block(lt, jb, None) for lt in range(n_lt)]
            decays = []
            for lt, (tot, pv) in enumerate(blocks):
                decay = decay_scr[lt]
                acc_scr[lt] += jnp.exp2(-decay) * pv
                decays.append(decay + tot)
                decay_scr[lt] = decays[lt]
            return jb - 1, any_live(decays)

        lax.while_loop(lambda state: jnp.logical_and(state[0] >= 0, state[1]), older,
                       (j_diag - 2, any_live(decays)))
        for lt in range(n_lt):
            o_ref[0, :, lanes[lt]] = acc_scr[lt].astype(o_ref.dtype)


def _attention(q, kt, vt, q_off):
    b, tq, _ = q.shape
    tk = kt.shape[2]
    width = ATTN_LANE_TILES * LANES
    bq = min(tq, ATTN_BLOCK)
    assert bq & (bq - 1) == 0 and tq % bq == 0 and tk % ATTN_BLOCK == 0
    assert q_off % ATTN_BLOCK == 0 and (bq == ATTN_BLOCK or tq == bq)
    assert q_off + tq <= tk
    r = lax.broadcasted_iota(jnp.int32, (ATTN_BLOCK, ATTN_BLOCK), 0)
    c = lax.broadcasted_iota(jnp.int32, (ATTN_BLOCK, ATTN_BLOCK), 1)
    tri = (r >= c).astype(BF16)
    visible = jnp.tile(c[:bq] < r[:bq], (2, 1))
    keep = visible.astype(F32)
    bias = jnp.where(visible, 0.0, CAUSAL_BIAS).astype(F32)
    return pl.pallas_call(
        functools.partial(_attn_kernel, q_off=q_off),
        grid=(b, D_MODEL // width, tq // bq),
        in_specs=[pl.BlockSpec((1, bq, width), lambda bi, hg, qi: (bi, qi, hg)),
                  pl.BlockSpec((1, width, tk), lambda bi, hg, qi: (bi, hg, 0)),
                  pl.BlockSpec((1, width, tk), lambda bi, hg, qi: (bi, hg, 0)),
                  _resident((ATTN_BLOCK, ATTN_BLOCK)),
                  _resident((2 * bq, ATTN_BLOCK)),
                  _resident((2 * bq, ATTN_BLOCK))],
        out_specs=pl.BlockSpec((1, bq, width), lambda bi, hg, qi: (bi, qi, hg)),
        out_shape=jax.ShapeDtypeStruct(q.shape, BF16),
        scratch_shapes=[pltpu.VMEM((ATTN_LANE_TILES, 2 * bq, LANES), BF16),
                        pltpu.VMEM((ATTN_LANE_TILES, bq, LANES), F32),
                        pltpu.VMEM((ATTN_LANE_TILES, bq, LANES), F32)],
        compiler_params=_params(3),
        name="attn",
    )(q, kt, vt, tri, keep, bias)


def _mix_kernel(oa_ref, ob_ref, ga_ref, gb_ref, x_ref, mod_ref, wa_ref, wb_ref, wo_ref, g_ref,
                x1_ref, h2_ref):
    ya = jnp.dot(oa_ref[...], wa_ref[...], preferred_element_type=F32)
    yb = jnp.dot(ob_ref[...], wb_ref[...], preferred_element_type=F32)
    merged = ga_ref[...].astype(F32) * ya + gb_ref[...].astype(F32) * yb
    y = jnp.dot(merged.astype(BF16), wo_ref[...], preferred_element_type=F32)
    tm = x_ref.shape[0]
    x1 = x_ref[...] + _mod_rows(mod_ref, 2, tm) * y
    x1_ref[...] = x1
    h2 = _rms_modulate(x1, g_ref[...], _mod_rows(mod_ref, 3, tm), _mod_rows(mod_ref, 4, tm))
    h2_ref[...] = h2.astype(BF16)


def _mix(oa, ob, ga, gb, x2d, mod, wa, wb, wo, g2, seq_len, tm):
    m = x2d.shape[0]
    spt, tiles_per_seq = _seq_tiling(seq_len, tm)
    row = pl.BlockSpec((tm, D_MODEL), lambda i: (i, 0))
    full = _resident((D_MODEL, D_MODEL))
    return pl.pallas_call(
        _mix_kernel,
        grid=(m // tm,),
        in_specs=[row, row, row, row, row,
                  pl.BlockSpec((spt, N_MOD, D_MODEL), lambda i: (i // tiles_per_seq, 0, 0)),
                  full, full, full, _resident((1, D_MODEL))],
        out_specs=[row, row],
        out_shape=[jax.ShapeDtypeStruct((m, D_MODEL), F32), jax.ShapeDtypeStruct((m, D_MODEL), BF16)],
        compiler_params=_params(1),
        name="mix",
    )(oa, ob, ga, gb, x2d, mod, wa, wb, wo, g2)


def _ffn_kernel(h_ref, x_ref, mod_ref, wu_ref, wf_ref, bf_ref, prev_ref, wd_ref, g_ref,
                y_ref, nf_ref, act_scr, hist_scr, *, tiles_per_seq):
    tm, d_ff = act_scr.shape
    _start_of_sequence(prev_ref, hist_scr, tiles_per_seq)

    for c in range(d_ff // MXU_WIDTH):
        sl = slice(c * MXU_WIDTH, (c + 1) * MXU_WIDTH)
        up = jnp.dot(h_ref[...], wu_ref[:, sl], preferred_element_type=F32)
        gate = jnp.dot(h_ref[...], wu_ref[:, slice(d_ff + sl.start, d_ff + sl.stop)],
                       preferred_element_type=F32)
        up_c = _conv_tile(up, sl, wf_ref, bf_ref, prev_ref, hist_scr, nf_ref)
        act_scr[:, sl] = (up_c * _sigmoid(up_c) * gate).astype(BF16)

    y = jnp.dot(act_scr[...], wd_ref[...], preferred_element_type=F32)
    x2 = x_ref[...] + _mod_rows(mod_ref, 5, tm) * y
    y_ref[...] = x2 * lax.rsqrt(jnp.mean(x2 * x2, axis=-1, keepdims=True) + EPS) * g_ref[...]


def _ffn(h2, x1, mod, w_up2, w_fconv, b_fconv, prev, w_down, g_final, seq_len, tm):
    m = x1.shape[0]
    d_ff = w_down.shape[0]
    assert d_ff % MXU_WIDTH == 0
    spt, tiles_per_seq = _seq_tiling(seq_len, tm)
    row = pl.BlockSpec((tm, D_MODEL), lambda i: (i, 0))
    state = (spt, CONV_WIDTH - 1, d_ff)
    return pl.pallas_call(
        functools.partial(_ffn_kernel, tiles_per_seq=tiles_per_seq),
        grid=(m // tm,),
        in_specs=[row, row,
                  pl.BlockSpec((spt, N_MOD, D_MODEL), lambda i: (i // tiles_per_seq, 0, 0)),
                  _resident((D_MODEL, 2 * d_ff)),
                  _resident((CONV_WIDTH, d_ff)),
                  _resident((1, d_ff)),
                  pl.BlockSpec(state, lambda i: (i // tiles_per_seq, 0, 0)),
                  _resident((d_ff, D_MODEL)),
                  _resident((1, D_MODEL))],
        out_specs=[row, pl.BlockSpec(state, lambda i: (i, 0, 0))],
        out_shape=[jax.ShapeDtypeStruct((m, D_MODEL), F32),
                   jax.ShapeDtypeStruct((m // tm * spt, CONV_WIDTH - 1, d_ff), F32)],
        scratch_shapes=[pltpu.VMEM((tm, d_ff), BF16),
                        pltpu.VMEM((CONV_WIDTH - 1, d_ff), F32)],
        compiler_params=_params(1),
        name="ffn",
    )(h2, x1, mod, w_up2, w_fconv, b_fconv, prev, w_down, g_final)


def _trunk(x, mod, past_k, past_v, conv_prev, ffn_prev, w, tm):
    bsz, t, _ = x.shape
    x2d = x.reshape(bsz * t, D_MODEL)
    q, kf, vf, kb, vb, ob, ga, gb, new_conv = _inproj(
        x2d, mod, w["g1"], w["w_kvt"], w["w_seg"], w["w_conv"], w["b_conv"], conv_prev, t, tm)
    if past_k is None:
        k_all, v_all, q_off = kb, vb, 0
    else:
        q_off = past_k.shape[1]
        pad = -(q_off + t) % ATTN_BLOCK
        cat = lambda past, new: jnp.pad(
            jnp.concatenate([past.transpose(0, 2, 3, 1).reshape(bsz, D_MODEL, q_off).astype(BF16),
                             new], axis=2),
            ((0, 0), (0, 0), (0, pad)))
        k_all, v_all = cat(past_k, kb), cat(past_v, vb)
    oa = _attention(q.reshape(bsz, t, D_MODEL), k_all, v_all, q_off).reshape(bsz * t, D_MODEL)
    x1, h2 = _mix(oa, ob, ga, gb, x2d, mod, w["wa"], w["wb"], w["wo"], w["g2"], t, tm)
    y, new_ffn = _ffn(h2, x1, mod, w["w_up2"], w["w_fconv"], w["b_fconv"], ffn_prev, w["w_down"],
                      w["g_final"], t, tm)
    heads = lambda a: a.reshape(bsz, N_HEADS, HEAD_DIM, t).transpose(0, 3, 1, 2)[None]
    last = lambda s: s.reshape(bsz, -1, CONV_WIDTH - 1, s.shape[-1])[None, :, -1]
    return (y.reshape(bsz, t, D_MODEL), heads(kf), heads(vf), last(new_conv), last(new_ffn))


def kernel(x_prompt, x_sample, cache_k, cache_v, state_conv, state_ffn_conv, c_prompt, c_sample,
           w_ada, b_ada, g_norm1, w_in, w_conv, b_conv, w_branch_a, w_branch_b, w_out,
           g_norm2, w_up, w_fconv, b_fconv, w_down, g_final):
    assert w_ada.shape[0] == 1, "one trunk layer"
    bp, bs = x_prompt.shape[0], x_sample.shape[0]
    d_ff = w_down.shape[1]
    w = {
        "g1": g_norm1[0].reshape(1, D_MODEL),
        "w_kvt": w_in[0][:, SEG_K * D_MODEL:(SEG_V + 1) * D_MODEL].astype(BF16)
        .reshape(D_MODEL, 2, D_MODEL).transpose(1, 2, 0),
        "w_seg": w_in[0].astype(BF16),
        "w_conv": w_conv[0], "b_conv": b_conv[0].reshape(1, -1),
        "wa": w_branch_a[0].astype(BF16), "wb": w_branch_b[0].astype(BF16),
        "wo": w_out[0].astype(BF16),
        "g2": g_norm2[0].reshape(1, D_MODEL),
        "w_up2": w_up[0].astype(BF16),
        "w_fconv": w_fconv[0], "b_fconv": b_fconv[0].reshape(1, -1),
        "w_down": w_down[0].astype(BF16),
        "g_final": g_final.reshape(1, D_MODEL),
    }
    mod = _ada(jnp.concatenate([c_prompt, c_sample], axis=0), w_ada[0], b_ada[0])
    mod = mod.reshape(bp + bs, N_MOD, D_MODEL)
    zeros = lambda width: jnp.zeros((bp, CONV_WIDTH - 1, width), x_prompt.dtype)
    yp, kp, vp, cp, fp = _trunk(x_prompt, mod[:bp], None, None, zeros(D_MODEL), zeros(d_ff), w, 512)
    ys, ks, vs, cs, fs = _trunk(x_sample, mod[bp:], cache_k[0], cache_v[0], state_conv[0],
                                state_ffn_conv[0], w, bs * x_sample.shape[1])
    return (yp, ys, kp, vp, cp, fp, ks, vs, cs, fs)
```

```python
import functools
import math

import jax
import jax.numpy as jnp
from jax import lax
from jax.experimental import pallas as pl
from jax.experimental.pallas import tpu as pltpu

F32 = jnp.float32
BF16 = jnp.bfloat16

D_MODEL = 1024
N_HEADS = 16
HEAD_DIM = 64
N_SEG = 8
SEG_Q, SEG_K, SEG_V, SEG_B, SEG_C, SEG_U, SEG_GA, SEG_GB = range(N_SEG)
N_MOD = 6
CONV_WIDTH = 3
EPS = 1e-6
LANES = 128
SUBLANES = 8
MXU_WIDTH = 256
ATTN_BLOCK = 256
ATTN_LANE_TILES = 8
VMEM_LIMIT = 56 * 1024 * 1024
LOG2E = math.log2(math.e)
Q_SCALE = HEAD_DIM ** -0.5 * LOG2E
DEAD_LOG2_DECAY = 105.0 * LOG2E
CAUSAL_BIAS = -1e30


def _params(n_axes):
    return pltpu.CompilerParams(dimension_semantics=("arbitrary",) * n_axes,
                                vmem_limit_bytes=VMEM_LIMIT)


def _resident(shape):
    return pl.BlockSpec(shape, lambda *_: (0,) * len(shape), pipeline_mode=pl.Buffered(1))


def _sigmoid(x):
    return 1.0 / (1.0 + jnp.exp(-x))


def _rms_modulate(x, gain, shift, scale):
    y = x * lax.rsqrt(jnp.mean(x * x, axis=-1, keepdims=True) + EPS) * gain
    return y * (1.0 + scale) + shift


def _shift_rows(u, fill):
    r = pltpu.roll(u, 1, 0)
    is_row0 = lax.broadcasted_iota(jnp.int32, (SUBLANES, u.shape[1]), 0) == 0
    top = jnp.where(is_row0, fill, r[:SUBLANES])
    return jnp.concatenate([top, r[SUBLANES:]], axis=0)


def _causal_conv(u, hist, w, b):
    u1 = _shift_rows(u, hist[1:2])
    u2 = _shift_rows(u1, hist[0:1])
    return b + w[0:1] * u2 + w[1:2] * u1 + w[2:3] * u


def _conv_tile(u, sl, w_ref, b_ref, prev_ref, hist_scr, new_ref):
    n_seq = prev_ref.shape[0]
    rows = u.shape[0] // n_seq
    out = []
    for s in range(n_seq):
        part = u[s * rows:(s + 1) * rows]
        hist = hist_scr[:, sl] if n_seq == 1 else prev_ref[s, :, sl]
        out.append(_causal_conv(part, hist, w_ref[:, sl], b_ref[:, sl]))
        tail = part[rows - 2:rows]
        if n_seq == 1:
            hist_scr[:, sl] = tail
        new_ref[s, :, sl] = tail
    return out[0] if n_seq == 1 else jnp.concatenate(out, axis=0)


def _start_of_sequence(prev_ref, hist_scr, tiles_per_seq):
    if prev_ref.shape[0] == 1:
        @pl.when(pl.program_id(0) % tiles_per_seq == 0)
        def _():
            hist_scr[...] = prev_ref[0]


def _mod_rows(mod_ref, k, tm):
    n_seq = mod_ref.shape[0]
    if n_seq == 1:
        return mod_ref[0, k:k + 1, :]
    return jnp.concatenate([jnp.broadcast_to(mod_ref[s, k:k + 1, :], (tm // n_seq, D_MODEL))
                            for s in range(n_seq)], axis=0)


def _seq_tiling(seq_len, tm):
    assert tm % seq_len == 0 or seq_len % tm == 0
    return max(1, tm // seq_len), max(1, seq_len // tm)


def _ada_kernel(c_ref, w_ref, b_ref, o_ref):
    c = c_ref[...]
    o_ref[...] = jnp.dot(c * _sigmoid(c), w_ref[...], preferred_element_type=F32) + b_ref[...]


def _ada(c, w_ada, b_ada):
    n, tn = c.shape[0], 1536
    return pl.pallas_call(
        _ada_kernel,
        grid=(N_MOD * D_MODEL // tn,),
        in_specs=[pl.BlockSpec((n, D_MODEL), lambda j: (0, 0)),
                  pl.BlockSpec((D_MODEL, tn), lambda j: (0, j)),
                  pl.BlockSpec((1, tn), lambda j: (0, j))],
        out_specs=pl.BlockSpec((n, tn), lambda j: (0, j)),
        out_shape=jax.ShapeDtypeStruct((n, N_MOD * D_MODEL), F32),
        compiler_params=_params(1),
        name="ada",
    )(c, w_ada, b_ada.reshape(1, -1))


def _inproj_kernel(x_ref, mod_ref, g_ref, wt_ref, w_ref, wc_ref, bc_ref, prev_ref,
                   q_ref, kf_ref, vf_ref, kb_ref, vb_ref, ob_ref, ga_ref, gb_ref, nc_ref,
                   hist_scr, *, tiles_per_seq):
    tm = x_ref.shape[0]
    _start_of_sequence(prev_ref, hist_scr, tiles_per_seq)
    h = _rms_modulate(x_ref[...], g_ref[...], _mod_rows(mod_ref, 0, tm), _mod_rows(mod_ref, 1, tm))
    hb = h.astype(BF16)

    for c in range(D_MODEL // MXU_WIDTH):
        sl = slice(c * MXU_WIDTH, (c + 1) * MXU_WIDTH)

        def proj(seg):
            cols = slice(seg * D_MODEL + sl.start, seg * D_MODEL + sl.stop)
            return jnp.dot(hb, w_ref[:, cols], preferred_element_type=F32)

        q_ref[:, sl] = (proj(SEG_Q) * Q_SCALE).astype(BF16)
        ga_ref[:, sl] = _sigmoid(proj(SEG_GA)).astype(BF16)
        gb_ref[:, sl] = _sigmoid(proj(SEG_GB)).astype(BF16)
        cu = proj(SEG_C) * proj(SEG_U)
        conv = _conv_tile(cu, sl, wc_ref, bc_ref, prev_ref, hist_scr, nc_ref)
        ob_ref[:, sl] = (proj(SEG_B) * conv).astype(BF16)

    n_seq = kf_ref.shape[0]
    for s, f_ref, b_ref in ((0, kf_ref, kb_ref), (1, vf_ref, vb_ref)):
        t = lax.dot_general(wt_ref[s], hb, (((1,), (1,)), ((), ())),
                            preferred_element_type=F32)
        for sq in range(n_seq):
            cols = slice(sq * (tm // n_seq), (sq + 1) * (tm // n_seq))
            f_ref[sq] = t[:, cols]
            b_ref[sq] = t[:, cols].astype(BF16)


def _inproj(x2d, mod, g1, w_kvt, w_seg, w_conv, b_conv, prev, seq_len, tm):
    m = x2d.shape[0]
    n_seq = m // seq_len
    spt, tiles_per_seq = _seq_tiling(seq_len, tm)
    row = pl.BlockSpec((tm, D_MODEL), lambda i: (i, 0))
    chan = pl.BlockSpec((spt, D_MODEL, tm // spt),
                        lambda i: (i // tiles_per_seq, 0, i % tiles_per_seq))
    act = lambda dt: jax.ShapeDtypeStruct((m, D_MODEL), dt)
    act_t = lambda dt: jax.ShapeDtypeStruct((n_seq, D_MODEL, seq_len), dt)
    state = (spt, CONV_WIDTH - 1, D_MODEL)
    return pl.pallas_call(
        functools.partial(_inproj_kernel, tiles_per_seq=tiles_per_seq),
        grid=(m // tm,),
        in_specs=[row,
                  pl.BlockSpec((spt, N_MOD, D_MODEL), lambda i: (i // tiles_per_seq, 0, 0)),
                  _resident((1, D_MODEL)),
                  _resident((2, D_MODEL, D_MODEL)),
                  _resident((D_MODEL, N_SEG * D_MODEL)),
                  _resident((CONV_WIDTH, D_MODEL)),
                  _resident((1, D_MODEL)),
                  pl.BlockSpec(state, lambda i: (i // tiles_per_seq, 0, 0))],
        out_specs=[row, chan, chan, chan, chan, row, row, row,
                   pl.BlockSpec(state, lambda i: (i, 0, 0))],
        out_shape=[act(BF16), act_t(F32), act_t(F32), act_t(BF16), act_t(BF16), act(BF16),
                   act(BF16), act(BF16),
                   jax.ShapeDtypeStruct((m // tm * spt, CONV_WIDTH - 1, D_MODEL), F32)],
        scratch_shapes=[pltpu.VMEM((CONV_WIDTH - 1, D_MODEL), F32)],
        compiler_params=_params(1),
        name="inproj",
    )(x2d, mod, g1, w_kvt, w_seg, w_conv, b_conv, prev)


def _attn_block(qs, kblk, vblk, tri, causal):
    bq = qs.shape[0] // 2
    lo = lax.broadcasted_iota(jnp.int32, (1, LANES), 1) < HEAD_DIM
    z = jnp.dot(qs, kblk, preferred_element_type=F32)
    sp = jnp.maximum(z, 0.0) + jnp.log(1.0 + jnp.exp2(-jnp.abs(z))) * LOG2E
    if causal is not None:
        sp = sp * causal[0]
    s = jnp.dot(sp.astype(BF16), tri, preferred_element_type=F32)
    x = z - s
    if causal is not None:
        x = x + causal[1]
    a = jnp.exp2(x)
    pv = lax.dot_general(a.astype(BF16), vblk, (((1,), (1,)), ((), ())),
                         preferred_element_type=F32)
    tot = jnp.sum(sp, axis=1, keepdims=True)
    return jnp.where(lo, tot[:bq], tot[bq:]), jnp.where(lo, pv[:bq], pv[bq:])


def _attn_kernel(q_ref, k_ref, v_ref, tri_ref, keep_ref, bias_ref, o_ref,
                 qs_scr, acc_scr, decay_scr, *, q_off):
    bq, bk = q_ref.shape[1], ATTN_BLOCK
    n_lt = q_ref.shape[2] // LANES
    j_diag = (q_off + pl.program_id(2) * bq) // bk
    lo = lax.broadcasted_iota(jnp.int32, (1, LANES), 1) < HEAD_DIM
    lanes = [slice(lt * LANES, (lt + 1) * LANES) for lt in range(n_lt)]
    for lt in range(n_lt):
        q = q_ref[0, :, lanes[lt]]
        qs_scr[lt, :bq] = jnp.where(lo, q, jnp.zeros_like(q))
        qs_scr[lt, bq:] = jnp.where(lo, jnp.zeros_like(q), q)

    def block(lt, jb, causal):
        start = pl.multiple_of(jb * bk, bk)
        return _attn_block(qs_scr[lt], k_ref[0, lanes[lt], pl.ds(start, bk)],
                           v_ref[0, lanes[lt], pl.ds(start, bk)], tri_ref[...], causal)

    def diag_block(lt):
        return block(lt, j_diag, (keep_ref[...], bias_ref[...]))

    def any_live(decays):
        m = functools.reduce(jnp.minimum, decays)
        m = jnp.min(jnp.min(m, axis=0, keepdims=True), axis=1, keepdims=True)
        return m[0, 0] < DEAD_LOG2_DECAY

    @pl.when(j_diag == 0)
    def _():
        for lt in range(n_lt):
            _, pv = diag_block(lt)
            o_ref[0, :, lanes[lt]] = pv.astype(o_ref.dtype)

    @pl.when(j_diag > 0)
    def _():
        newest = [(diag_block(lt), block(lt, j_diag - 1, None)) for lt in range(n_lt)]
        accs = [pv0 + jnp.exp2(-tot0) * pv1 for (tot0, pv0), (_, pv1) in newest]
        decays = [tot0 + tot1 for (tot0, _), (tot1, _) in newest]
        for lt in range(n_lt):
            o_ref[0, :, lanes[lt]] = accs[lt].astype(o_ref.dtype)

        @pl.when(jnp.logical_and(j_diag >= 2, any_live(decays)))
        def _():
            for lt in range(n_lt):
                acc_scr[lt] = accs[lt]
                decay_scr[lt] = decays[lt]

            def older(state):
                jb, _ = state
                blocks = [block(lt, jb, None) for lt in range(n_lt)]
                decays = []
                for lt, (tot, pv) in enumerate(blocks):
                    decay = decay_scr[lt]
                    acc_scr[lt] += jnp.exp2(-decay) * pv
                    decays.append(decay + tot)
                    decay_scr[lt] = decays[lt]
                return jb - 1, any_live(decays)

            lax.while_loop(lambda state: jnp.logical_and(state[0] >= 0, state[1]), older,
                           (j_diag - 2, jnp.bool_(True)))
            for lt in range(n_lt):
                o_ref[0, :, lanes[lt]] = acc_scr[lt].astype(o_ref.dtype)


def _attention(q, kt, vt, q_off):
    b, tq, _ = q.shape
    tk = kt.shape[2]
    width = ATTN_LANE_TILES * LANES
    bq = min(tq, ATTN_BLOCK)
    assert bq & (bq - 1) == 0 and tq % bq == 0 and tk % ATTN_BLOCK == 0
    assert q_off % ATTN_BLOCK == 0 and (bq == ATTN_BLOCK or tq == bq)
    assert q_off + tq <= tk
    r = lax.broadcasted_iota(jnp.int32, (ATTN_BLOCK, ATTN_BLOCK), 0)
    c = lax.broadcasted_iota(jnp.int32, (ATTN_BLOCK, ATTN_BLOCK), 1)
    tri = (r >= c).astype(BF16)
    visible = jnp.tile(c[:bq] < r[:bq], (2, 1))
    keep = visible.astype(F32)
    bias = jnp.where(visible, 0.0, CAUSAL_BIAS).astype(F32)
    return pl.pallas_call(
        functools.partial(_attn_kernel, q_off=q_off),
        grid=(b, D_MODEL // width, tq // bq),
        in_specs=[pl.BlockSpec((1, bq, width), lambda bi, hg, qi: (bi, qi, hg)),
                  pl.BlockSpec((1, width, tk), lambda bi, hg, qi: (bi, hg, 0)),
                  pl.BlockSpec((1, width, tk), lambda bi, hg, qi: (bi, hg, 0)),
                  _resident((ATTN_BLOCK, ATTN_BLOCK)),
                  _resident((2 * bq, ATTN_BLOCK)),
                  _resident((2 * bq, ATTN_BLOCK))],
        out_specs=pl.BlockSpec((1, bq, width), lambda bi, hg, qi: (bi, qi, hg)),
        out_shape=jax.ShapeDtypeStruct(q.shape, BF16),
        scratch_shapes=[pltpu.VMEM((ATTN_LANE_TILES, 2 * bq, LANES), BF16),
                        pltpu.VMEM((ATTN_LANE_TILES, bq, LANES), F32),
                        pltpu.VMEM((ATTN_LANE_TILES, bq, LANES), F32)],
        compiler_params=_params(3),
        name="attn",
    )(q, kt, vt, tri, keep, bias)


def _mix_kernel(oa_ref, ob_ref, ga_ref, gb_ref, x_ref, mod_ref, wa_ref, wb_ref, wo_ref, g_ref,
                x1_ref, h2_ref):
    ya = jnp.dot(oa_ref[...], wa_ref[...], preferred_element_type=F32)
    yb = jnp.dot(ob_ref[...], wb_ref[...], preferred_element_type=F32)
    merged = ga_ref[...].astype(F32) * ya + gb_ref[...].astype(F32) * yb
    y = jnp.dot(merged.astype(BF16), wo_ref[...], preferred_element_type=F32)
    tm = x_ref.shape[0]
    x1 = x_ref[...] + _mod_rows(mod_ref, 2, tm) * y
    x1_ref[...] = x1
    h2 = _rms_modulate(x1, g_ref[...], _mod_rows(mod_ref, 3, tm), _mod_rows(mod_ref, 4, tm))
    h2_ref[...] = h2.astype(BF16)


def _mix(oa, ob, ga, gb, x2d, mod, wa, wb, wo, g2, seq_len, tm):
    m = x2d.shape[0]
    spt, tiles_per_seq = _seq_tiling(seq_len, tm)
    row = pl.BlockSpec((tm, D_MODEL), lambda i: (i, 0))
    full = _resident((D_MODEL, D_MODEL))
    return pl.pallas_call(
        _mix_kernel,
        grid=(m // tm,),
        in_specs=[row, row, row, row, row,
                  pl.BlockSpec((spt, N_MOD, D_MODEL), lambda i: (i // tiles_per_seq, 0, 0)),
                  full, full, full, _resident((1, D_MODEL))],
        out_specs=[row, row],
        out_shape=[jax.ShapeDtypeStruct((m, D_MODEL), F32), jax.ShapeDtypeStruct((m, D_MODEL), BF16)],
        compiler_params=_params(1),
        name="mix",
    )(oa, ob, ga, gb, x2d, mod, wa, wb, wo, g2)


def _ffn_kernel(h_ref, x_ref, mod_ref, wu_ref, wf_ref, bf_ref, prev_ref, wd_ref, g_ref,
                y_ref, nf_ref, act_scr, hist_scr, *, tiles_per_seq):
    tm, d_ff = act_scr.shape
    _start_of_sequence(prev_ref, hist_scr, tiles_per_seq)

    for c in range(d_ff // MXU_WIDTH):
        sl = slice(c * MXU_WIDTH, (c + 1) * MXU_WIDTH)
        up = jnp.dot(h_ref[...], wu_ref[:, sl], preferred_element_type=F32)
        gate = jnp.dot(h_ref[...], wu_ref[:, slice(d_ff + sl.start, d_ff + sl.stop)],
                       preferred_element_type=F32)
        up_c = _conv_tile(up, sl, wf_ref, bf_ref, prev_ref, hist_scr, nf_ref)
        act_scr[:, sl] = (up_c * _sigmoid(up_c) * gate).astype(BF16)

    y = jnp.dot(act_scr[...], wd_ref[...], preferred_element_type=F32)
    x2 = x_ref[...] + _mod_rows(mod_ref, 5, tm) * y
    y_ref[...] = x2 * lax.rsqrt(jnp.mean(x2 * x2, axis=-1, keepdims=True) + EPS) * g_ref[...]


def _ffn(h2, x1, mod, w_up2, w_fconv, b_fconv, prev, w_down, g_final, seq_len, tm):
    m = x1.shape[0]
    d_ff = w_down.shape[0]
    assert d_ff % MXU_WIDTH == 0
    spt, tiles_per_seq = _seq_tiling(seq_len, tm)
    row = pl.BlockSpec((tm, D_MODEL), lambda i: (i, 0))
    state = (spt, CONV_WIDTH - 1, d_ff)
    return pl.pallas_call(
        functools.partial(_ffn_kernel, tiles_per_seq=tiles_per_seq),
        grid=(m // tm,),
        in_specs=[row, row,
                  pl.BlockSpec((spt, N_MOD, D_MODEL), lambda i: (i // tiles_per_seq, 0, 0)),
                  _resident((D_MODEL, 2 * d_ff)),
                  _resident((CONV_WIDTH, d_ff)),
                  _resident((1, d_ff)),
                  pl.BlockSpec(state, lambda i: (i // tiles_per_seq, 0, 0)),
                  _resident((d_ff, D_MODEL)),
                  _resident((1, D_MODEL))],
        out_specs=[row, pl.BlockSpec(state, lambda i: (i, 0, 0))],
        out_shape=[jax.ShapeDtypeStruct((m, D_MODEL), F32),
                   jax.ShapeDtypeStruct((m // tm * spt, CONV_WIDTH - 1, d_ff), F32)],
        scratch_shapes=[pltpu.VMEM((tm, d_ff), BF16),
                        pltpu.VMEM((CONV_WIDTH - 1, d_ff), F32)],
        compiler_params=_params(1),
        name="ffn",
    )(h2, x1, mod, w_up2, w_fconv, b_fconv, prev, w_down, g_final)


def _trunk(x, mod, past_k, past_v, conv_prev, ffn_prev, w, tm):
    bsz, t, _ = x.shape
    x2d = x.reshape(bsz * t, D_MODEL)
    q, kf, vf, kb, vb, ob, ga, gb, new_conv = _inproj(
        x2d, mod, w["g1"], w["w_kvt"], w["w_seg"], w["w_conv"], w["b_conv"], conv_prev, t, tm)
    if past_k is None:
        k_all, v_all, q_off = kb, vb, 0
    else:
        q_off = past_k.shape[1]
        pad = -(q_off + t) % ATTN_BLOCK
        cat = lambda past, new: jnp.pad(
            jnp.concatenate([past.transpose(0, 2, 3, 1).reshape(bsz, D_MODEL, q_off).astype(BF16),
                             new], axis=2),
            ((0, 0), (0, 0), (0, pad)))
        k_all, v_all = cat(past_k, kb), cat(past_v, vb)
    oa = _attention(q.reshape(bsz, t, D_MODEL), k_all, v_all, q_off).reshape(bsz * t, D_MODEL)
    tm2 = 2 * tm if past_k is None else tm
    x1, h2 = _mix(oa, ob, ga, gb, x2d, mod, w["wa"], w["wb"], w["wo"], w["g2"], t, tm2)
    y, new_ffn = _ffn(h2, x1, mod, w["w_up2"], w["w_fconv"], w["b_fconv"], ffn_prev, w["w_down"],
                      w["g_final"], t, tm2)
    heads = lambda a: a.reshape(bsz, N_HEADS, HEAD_DIM, t).transpose(0, 3, 1, 2)[None]
    last = lambda s: s.reshape(bsz, -1, CONV_WIDTH - 1, s.shape[-1])[None, :, -1]
    return (y.reshape(bsz, t, D_MODEL), heads(kf), heads(vf), last(new_conv), last(new_ffn))


def kernel(x_prompt, x_sample, cache_k, cache_v, state_conv, state_ffn_conv, c_prompt, c_sample,
           w_ada, b_ada, g_norm1, w_in, w_conv, b_conv, w_branch_a, w_branch_b, w_out,
           g_norm2, w_up, w_fconv, b_fconv, w_down, g_final):
    assert w_ada.shape[0] == 1, "one trunk layer"
    bp, bs = x_prompt.shape[0], x_sample.shape[0]
    d_ff = w_down.shape[1]
    w = {
        "g1": g_norm1[0].reshape(1, D_MODEL),
        "w_kvt": w_in[0][:, SEG_K * D_MODEL:(SEG_V + 1) * D_MODEL].astype(BF16)
        .reshape(D_MODEL, 2, D_MODEL).transpose(1, 2, 0),
        "w_seg": w_in[0].astype(BF16),
        "w_conv": w_conv[0], "b_conv": b_conv[0].reshape(1, -1),
        "wa": w_branch_a[0].astype(BF16), "wb": w_branch_b[0].astype(BF16),
        "wo": w_out[0].astype(BF16),
        "g2": g_norm2[0].reshape(1, D_MODEL),
        "w_up2": w_up[0].astype(BF16),
        "w_fconv": w_fconv[0], "b_fconv": b_fconv[0].reshape(1, -1),
        "w_down": w_down[0].astype(BF16),
        "g_final": g_final.reshape(1, D_MODEL),
    }
    mod = _ada(jnp.concatenate([c_prompt, c_sample], axis=0), w_ada[0], b_ada[0])
    mod = mod.reshape(bp + bs, N_MOD, D_MODEL)
    zeros = lambda width: jnp.zeros((bp, CONV_WIDTH - 1, width), x_prompt.dtype)
    yp, kp, vp, cp, fp = _trunk(x_prompt, mod[:bp], None, None, zeros(D_MODEL), zeros(d_ff), w, 512)
    ys, ks, vs, cs, fs = _trunk(x_sample, mod[bp:], cache_k[0], cache_v[0], state_conv[0],
                                state_ffn_conv[0], w, bs * x_sample.shape[1])
    return (yp, ys, kp, vp, cp, fp, ks, vs, cs, fs)
```

```python
import functools
import math

import jax
import jax.numpy as jnp
from jax import lax
from jax.experimental import pallas as pl
from jax.experimental.pallas import tpu as pltpu

F32 = jnp.float32
BF16 = jnp.bfloat16

D_MODEL = 1024
N_HEADS = 16
HEAD_DIM = 64
N_SEG = 8
SEG_Q, SEG_K, SEG_V, SEG_B, SEG_C, SEG_U, SEG_GA, SEG_GB = range(N_SEG)
N_MOD = 6
CONV_WIDTH = 3
EPS = 1e-6
LANES = 128
SUBLANES = 8
MXU_WIDTH = 256
ATTN_BLOCK = 256
ATTN_LANE_TILES = 8
VMEM_LIMIT = 56 * 1024 * 1024
PROMPT_TILE = 512
PROMPT_TILE_WIDE = 1024
LOG2E = math.log2(math.e)
Q_SCALE = HEAD_DIM ** -0.5 * LOG2E
DEAD_LOG2_DECAY = 105.0 * LOG2E
CAUSAL_BIAS = -1e30


def _params(n_axes):
    return pltpu.CompilerParams(dimension_semantics=("arbitrary",) * n_axes,
                                vmem_limit_bytes=VMEM_LIMIT)


def _resident(shape):
    return pl.BlockSpec(shape, lambda *_: (0,) * len(shape), pipeline_mode=pl.Buffered(1))


def _sigmoid(x):
    return 1.0 / (1.0 + jnp.exp(-x))


def _rms_modulate(x, gain, shift, scale):
    y = x * lax.rsqrt(jnp.mean(x * x, axis=-1, keepdims=True) + EPS) * gain
    return y * (1.0 + scale) + shift


def _shift_rows(u, fill):
    r = pltpu.roll(u, 1, 0)
    is_row0 = lax.broadcasted_iota(jnp.int32, (SUBLANES, u.shape[1]), 0) == 0
    top = jnp.where(is_row0, fill, r[:SUBLANES])
    return jnp.concatenate([top, r[SUBLANES:]], axis=0)


def _causal_conv(u, hist, w, b):
    u1 = _shift_rows(u, hist[1:2])
    u2 = _shift_rows(u1, hist[0:1])
    return b + w[0:1] * u2 + w[1:2] * u1 + w[2:3] * u


def _conv_tile(u, sl, w_ref, b_ref, prev_ref, hist_scr, new_ref):
    n_seq = prev_ref.shape[0]
    rows = u.shape[0] // n_seq
    out = []
    for s in range(n_seq):
        part = u[s * rows:(s + 1) * rows]
        hist = hist_scr[:, sl] if n_seq == 1 else prev_ref[s, :, sl]
        out.append(_causal_conv(part, hist, w_ref[:, sl], b_ref[:, sl]))
        tail = part[rows - 2:rows]
        if n_seq == 1:
            hist_scr[:, sl] = tail
        new_ref[s, :, sl] = tail
    return out[0] if n_seq == 1 else jnp.concatenate(out, axis=0)


def _start_of_sequence(prev_ref, hist_scr, tiles_per_seq):
    if prev_ref.shape[0] == 1:
        @pl.when(pl.program_id(0) % tiles_per_seq == 0)
        def _():
            hist_scr[...] = prev_ref[0]


def _mod_rows(mod_ref, k, tm):
    n_seq = mod_ref.shape[0]
    if n_seq == 1:
        return mod_ref[0, k:k + 1, :]
    return jnp.concatenate([jnp.broadcast_to(mod_ref[s, k:k + 1, :], (tm // n_seq, D_MODEL))
                            for s in range(n_seq)], axis=0)


def _seq_tiling(seq_len, tm):
    assert tm % seq_len == 0 or seq_len % tm == 0
    return max(1, tm // seq_len), max(1, seq_len // tm)


def _ada_kernel(c_ref, w_ref, b_ref, o_ref):
    c = c_ref[...]
    o_ref[...] = jnp.dot(c * _sigmoid(c), w_ref[...], preferred_element_type=F32) + b_ref[...]


def _ada(c, w_ada, b_ada):
    n, tn = c.shape[0], 1536
    return pl.pallas_call(
        _ada_kernel,
        grid=(N_MOD * D_MODEL // tn,),
        in_specs=[pl.BlockSpec((n, D_MODEL), lambda j: (0, 0)),
                  pl.BlockSpec((D_MODEL, tn), lambda j: (0, j)),
                  pl.BlockSpec((1, tn), lambda j: (0, j))],
        out_specs=pl.BlockSpec((n, tn), lambda j: (0, j)),
        out_shape=jax.ShapeDtypeStruct((n, N_MOD * D_MODEL), F32),
        compiler_params=_params(1),
        name="ada",
    )(c, w_ada, b_ada.reshape(1, -1))


def _inproj_kernel(x_ref, mod_ref, g_ref, wt_ref, w_ref, wc_ref, bc_ref, prev_ref,
                   q_ref, kf_ref, vf_ref, kb_ref, vb_ref, ob_ref, ga_ref, gb_ref, nc_ref,
                   hist_scr, *, tiles_per_seq):
    tm = x_ref.shape[0]
    _start_of_sequence(prev_ref, hist_scr, tiles_per_seq)
    h = _rms_modulate(x_ref[...], g_ref[...], _mod_rows(mod_ref, 0, tm), _mod_rows(mod_ref, 1, tm))
    hb = h.astype(BF16)

    for c in range(D_MODEL // MXU_WIDTH):
        sl = slice(c * MXU_WIDTH, (c + 1) * MXU_WIDTH)

        def proj(seg):
            cols = slice(seg * D_MODEL + sl.start, seg * D_MODEL + sl.stop)
            return jnp.dot(hb, w_ref[:, cols], preferred_element_type=F32)

        q_ref[:, sl] = (proj(SEG_Q) * Q_SCALE).astype(BF16)
        ga_ref[:, sl] = _sigmoid(proj(SEG_GA)).astype(BF16)
        gb_ref[:, sl] = _sigmoid(proj(SEG_GB)).astype(BF16)
        cu = proj(SEG_C) * proj(SEG_U)
        conv = _conv_tile(cu, sl, wc_ref, bc_ref, prev_ref, hist_scr, nc_ref)
        ob_ref[:, sl] = (proj(SEG_B) * conv).astype(BF16)

    n_seq = kf_ref.shape[0]
    for s, f_ref, b_ref in ((0, kf_ref, kb_ref), (1, vf_ref, vb_ref)):
        t = lax.dot_general(wt_ref[s], hb, (((1,), (1,)), ((), ())),
                            preferred_element_type=F32)
        for sq in range(n_seq):
            cols = slice(sq * (tm // n_seq), (sq + 1) * (tm // n_seq))
            f_ref[sq] = t[:, cols]
            b_ref[sq] = t[:, cols].astype(BF16)


def _inproj(x2d, mod, g1, w_kvt, w_seg, w_conv, b_conv, prev, seq_len, tm):
    m = x2d.shape[0]
    n_seq = m // seq_len
    spt, tiles_per_seq = _seq_tiling(seq_len, tm)
    row = pl.BlockSpec((tm, D_MODEL), lambda i: (i, 0))
    chan = pl.BlockSpec((spt, D_MODEL, tm // spt),
                        lambda i: (i // tiles_per_seq, 0, i % tiles_per_seq))
    act = lambda dt: jax.ShapeDtypeStruct((m, D_MODEL), dt)
    act_t = lambda dt: jax.ShapeDtypeStruct((n_seq, D_MODEL, seq_len), dt)
    state = (spt, CONV_WIDTH - 1, D_MODEL)
    return pl.pallas_call(
        functools.partial(_inproj_kernel, tiles_per_seq=tiles_per_seq),
        grid=(m // tm,),
        in_specs=[row,
                  pl.BlockSpec((spt, N_MOD, D_MODEL), lambda i: (i // tiles_per_seq, 0, 0)),
                  _resident((1, D_MODEL)),
                  _resident((2, D_MODEL, D_MODEL)),
                  _resident((D_MODEL, N_SEG * D_MODEL)),
                  _resident((CONV_WIDTH, D_MODEL)),
                  _resident((1, D_MODEL)),
                  pl.BlockSpec(state, lambda i: (i // tiles_per_seq, 0, 0))],
        out_specs=[row, chan, chan, chan, chan, row, row, row,
                   pl.BlockSpec(state, lambda i: (i, 0, 0))],
        out_shape=[act(BF16), act_t(F32), act_t(F32), act_t(BF16), act_t(BF16), act(BF16),
                   act(BF16), act(BF16),
                   jax.ShapeDtypeStruct((m // tm * spt, CONV_WIDTH - 1, D_MODEL), F32)],
        scratch_shapes=[pltpu.VMEM((CONV_WIDTH - 1, D_MODEL), F32)],
        compiler_params=_params(1),
        name="inproj",
    )(x2d, mod, g1, w_kvt, w_seg, w_conv, b_conv, prev)


def _attn_block(qs, kblk, vblk, tri, causal):
    bq = qs.shape[0] // 2
    lo = lax.broadcasted_iota(jnp.int32, (1, LANES), 1) < HEAD_DIM
    z = jnp.dot(qs, kblk, preferred_element_type=F32)
    sp = jnp.maximum(z, 0.0) + jnp.log(1.0 + jnp.exp2(-jnp.abs(z))) * LOG2E
    if causal is not None:
        sp = sp * causal[0]
    s = jnp.dot(sp.astype(BF16), tri, preferred_element_type=F32)
    x = z - s
    if causal is not None:
        x = x + causal[1]
    a = jnp.exp2(x)
    pv = lax.dot_general(a.astype(BF16), vblk, (((1,), (1,)), ((), ())),
                         preferred_element_type=F32)
    tot = jnp.sum(sp, axis=1, keepdims=True)
    return jnp.where(lo, tot[:bq], tot[bq:]), jnp.where(lo, pv[:bq], pv[bq:])


def _attn_kernel(q_ref, k_ref, v_ref, tri_ref, keep_ref, bias_ref, o_ref,
                 qs_scr, acc_scr, decay_scr, *, q_off):
    bq, bk = q_ref.shape[1], ATTN_BLOCK
    n_lt = q_ref.shape[2] // LANES
    j_diag = (q_off + pl.program_id(2) * bq) // bk
    lo = lax.broadcasted_iota(jnp.int32, (1, LANES), 1) < HEAD_DIM
    lanes = [slice(lt * LANES, (lt + 1) * LANES) for lt in range(n_lt)]
    for lt in range(n_lt):
        q = q_ref[0, :, lanes[lt]]
        qs_scr[lt, :bq] = jnp.where(lo, q, jnp.zeros_like(q))
        qs_scr[lt, bq:] = jnp.where(lo, jnp.zeros_like(q), q)

    def block(lt, jb, causal):
        start = pl.multiple_of(jb * bk, bk)
        return _attn_block(qs_scr[lt], k_ref[0, lanes[lt], pl.ds(start, bk)],
                           v_ref[0, lanes[lt], pl.ds(start, bk)], tri_ref[...], causal)

    def diag_block(lt):
        return block(lt, j_diag, (keep_ref[...], bias_ref[...]))

    def any_live(decays):
        m = functools.reduce(jnp.minimum, decays)
        m = jnp.min(jnp.min(m, axis=0, keepdims=True), axis=1, keepdims=True)
        return m[0, 0] < DEAD_LOG2_DECAY

    @pl.when(j_diag == 0)
    def _():
        for lt in range(n_lt):
            _, pv = diag_block(lt)
            o_ref[0, :, lanes[lt]] = pv.astype(o_ref.dtype)

    @pl.when(j_diag > 0)
    def _():
        newest = [(diag_block(lt), block(lt, j_diag - 1, None)) for lt in range(n_lt)]
        accs = [pv0 + jnp.exp2(-tot0) * pv1 for (tot0, pv0), (_, pv1) in newest]
        decays = [tot0 + tot1 for (tot0, _), (tot1, _) in newest]
        for lt in range(n_lt):
            o_ref[0, :, lanes[lt]] = accs[lt].astype(o_ref.dtype)

        @pl.when(jnp.logical_and(j_diag >= 2, any_live(decays)))
        def _():
            for lt in range(n_lt):
                acc_scr[lt] = accs[lt]
                decay_scr[lt] = decays[lt]

            def older(state):
                jb, _ = state
                blocks = [block(lt, jb, None) for lt in range(n_lt)]
                decays = []
                for lt, (tot, pv) in enumerate(blocks):
                    decay = decay_scr[lt]
                    acc_scr[lt] += jnp.exp2(-decay) * pv
                    decays.append(decay + tot)
                    decay_scr[lt] = decays[lt]
                return jb - 1, any_live(decays)

            lax.while_loop(lambda state: jnp.logical_and(state[0] >= 0, state[1]), older,
                           (j_diag - 2, jnp.bool_(True)))
            for lt in range(n_lt):
                o_ref[0, :, lanes[lt]] = acc_scr[lt].astype(o_ref.dtype)


def _attention(q, kt, vt, q_off):
    b, tq, _ = q.shape
    tk = kt.shape[2]
    width = ATTN_LANE_TILES * LANES
    bq = min(tq, ATTN_BLOCK)
    assert bq & (bq - 1) == 0 and tq % bq == 0 and tk % ATTN_BLOCK == 0
    assert q_off % ATTN_BLOCK == 0 and (bq == ATTN_BLOCK or tq == bq)
    assert q_off + tq <= tk
    r = lax.broadcasted_iota(jnp.int32, (ATTN_BLOCK, ATTN_BLOCK), 0)
    c = lax.broadcasted_iota(jnp.int32, (ATTN_BLOCK, ATTN_BLOCK), 1)
    tri = (r >= c).astype(BF16)
    visible = jnp.tile(c[:bq] < r[:bq], (2, 1))
    keep = visible.astype(F32)
    bias = jnp.where(visible, 0.0, CAUSAL_BIAS).astype(F32)
    return pl.pallas_call(
        functools.partial(_attn_kernel, q_off=q_off),
        grid=(b, D_MODEL // width, tq // bq),
        in_specs=[pl.BlockSpec((1, bq, width), lambda bi, hg, qi: (bi, qi, hg)),
                  pl.BlockSpec((1, width, tk), lambda bi, hg, qi: (bi, hg, 0)),
                  pl.BlockSpec((1, width, tk), lambda bi, hg, qi: (bi, hg, 0)),
                  _resident((ATTN_BLOCK, ATTN_BLOCK)),
                  _resident((2 * bq, ATTN_BLOCK)),
                  _resident((2 * bq, ATTN_BLOCK))],
        out_specs=pl.BlockSpec((1, bq, width), lambda bi, hg, qi: (bi, qi, hg)),
        out_shape=jax.ShapeDtypeStruct(q.shape, BF16),
        scratch_shapes=[pltpu.VMEM((ATTN_LANE_TILES, 2 * bq, LANES), BF16),
                        pltpu.VMEM((ATTN_LANE_TILES, bq, LANES), F32),
                        pltpu.VMEM((ATTN_LANE_TILES, bq, LANES), F32)],
        compiler_params=_params(3),
        name="attn",
    )(q, kt, vt, tri, keep, bias)


def _mix_kernel(oa_ref, ob_ref, ga_ref, gb_ref, x_ref, mod_ref, wa_ref, wb_ref, wo_ref, g_ref,
                x1_ref, h2_ref):
    ya = jnp.dot(oa_ref[...], wa_ref[...], preferred_element_type=F32)
    yb = jnp.dot(ob_ref[...], wb_ref[...], preferred_element_type=F32)
    merged = ga_ref[...].astype(F32) * ya + gb_ref[...].astype(F32) * yb
    y = jnp.dot(merged.astype(BF16), wo_ref[...], preferred_element_type=F32)
    tm = x_ref.shape[0]
    x1 = x_ref[...] + _mod_rows(mod_ref, 2, tm) * y
    x1_ref[...] = x1
    h2 = _rms_modulate(x1, g_ref[...], _mod_rows(mod_ref, 3, tm), _mod_rows(mod_ref, 4, tm))
    h2_ref[...] = h2.astype(BF16)


def _mix(oa, ob, ga, gb, x2d, mod, wa, wb, wo, g2, seq_len, tm):
    m = x2d.shape[0]
    spt, tiles_per_seq = _seq_tiling(seq_len, tm)
    row = pl.BlockSpec((tm, D_MODEL), lambda i: (i, 0))
    full = _resident((D_MODEL, D_MODEL))
    return pl.pallas_call(
        _mix_kernel,
        grid=(m // tm,),
        in_specs=[row, row, row, row, row,
                  pl.BlockSpec((spt, N_MOD, D_MODEL), lambda i: (i // tiles_per_seq, 0, 0)),
                  full, full, full, _resident((1, D_MODEL))],
        out_specs=[row, row],
        out_shape=[jax.ShapeDtypeStruct((m, D_MODEL), F32), jax.ShapeDtypeStruct((m, D_MODEL), BF16)],
        compiler_params=_params(1),
        name="mix",
    )(oa, ob, ga, gb, x2d, mod, wa, wb, wo, g2)


def _ffn_kernel(h_ref, x_ref, mod_ref, wu_ref, wf_ref, bf_ref, prev_ref, wd_ref, g_ref,
                y_ref, nf_ref, act_scr, hist_scr, *, tiles_per_seq):
    tm, d_ff = act_scr.shape
    _start_of_sequence(prev_ref, hist_scr, tiles_per_seq)

    for c in range(d_ff // MXU_WIDTH):
        sl = slice(c * MXU_WIDTH, (c + 1) * MXU_WIDTH)
        up = jnp.dot(h_ref[...], wu_ref[:, sl], preferred_element_type=F32)
        gate = jnp.dot(h_ref[...], wu_ref[:, slice(d_ff + sl.start, d_ff + sl.stop)],
                       preferred_element_type=F32)
        up_c = _conv_tile(up, sl, wf_ref, bf_ref, prev_ref, hist_scr, nf_ref)
        act_scr[:, sl] = (up_c * _sigmoid(up_c) * gate).astype(BF16)

    y = jnp.dot(act_scr[...], wd_ref[...], preferred_element_type=F32)
    x2 = x_ref[...] + _mod_rows(mod_ref, 5, tm) * y
    y_ref[...] = x2 * lax.rsqrt(jnp.mean(x2 * x2, axis=-1, keepdims=True) + EPS) * g_ref[...]


def _ffn(h2, x1, mod, w_up2, w_fconv, b_fconv, prev, w_down, g_final, seq_len, tm):
    m = x1.shape[0]
    d_ff = w_down.shape[0]
    assert d_ff % MXU_WIDTH == 0
    spt, tiles_per_seq = _seq_tiling(seq_len, tm)
    row = pl.BlockSpec((tm, D_MODEL), lambda i: (i, 0))
    state = (spt, CONV_WIDTH - 1, d_ff)
    return pl.pallas_call(
        functools.partial(_ffn_kernel, tiles_per_seq=tiles_per_seq),
        grid=(m // tm,),
        in_specs=[row, row,
                  pl.BlockSpec((spt, N_MOD, D_MODEL), lambda i: (i // tiles_per_seq, 0, 0)),
                  _resident((D_MODEL, 2 * d_ff)),
                  _resident((CONV_WIDTH, d_ff)),
                  _resident((1, d_ff)),
                  pl.BlockSpec(state, lambda i: (i // tiles_per_seq, 0, 0)),
                  _resident((d_ff, D_MODEL)),
                  _resident((1, D_MODEL))],
        out_specs=[row, pl.BlockSpec(state, lambda i: (i, 0, 0))],
        out_shape=[jax.ShapeDtypeStruct((m, D_MODEL), F32),
                   jax.ShapeDtypeStruct((m // tm * spt, CONV_WIDTH - 1, d_ff), F32)],
        scratch_shapes=[pltpu.VMEM((tm, d_ff), BF16),
                        pltpu.VMEM((CONV_WIDTH - 1, d_ff), F32)],
        compiler_params=_params(1),
        name="ffn",
    )(h2, x1, mod, w_up2, w_fconv, b_fconv, prev, w_down, g_final)


def _trunk(x, mod, past_k, past_v, conv_prev, ffn_prev, w, tm, tm_wide):
    bsz, t, _ = x.shape
    x2d = x.reshape(bsz * t, D_MODEL)
    q, kf, vf, kb, vb, ob, ga, gb, new_conv = _inproj(
        x2d, mod, w["g1"], w["w_kvt"], w["w_seg"], w["w_conv"], w["b_conv"], conv_prev, t, tm)
    if past_k is None:
        k_all, v_all, q_off = kb, vb, 0
    else:
        q_off = past_k.shape[1]
        pad = -(q_off + t) % ATTN_BLOCK
        cat = lambda past, new: jnp.pad(
            jnp.concatenate([past.transpose(0, 2, 3, 1).reshape(bsz, D_MODEL, q_off).astype(BF16),
                             new], axis=2),
            ((0, 0), (0, 0), (0, pad)))
        k_all, v_all = cat(past_k, kb), cat(past_v, vb)
    oa = _attention(q.reshape(bsz, t, D_MODEL), k_all, v_all, q_off).reshape(bsz * t, D_MODEL)
    x1, h2 = _mix(oa, ob, ga, gb, x2d, mod, w["wa"], w["wb"], w["wo"], w["g2"], t, tm_wide)
    y, new_ffn = _ffn(h2, x1, mod, w["w_up2"], w["w_fconv"], w["b_fconv"], ffn_prev, w["w_down"],
                      w["g_final"], t, tm_wide)
    heads = lambda a: a.reshape(bsz, N_HEADS, HEAD_DIM, t).transpose(0, 3, 1, 2)[None]
    last = lambda s: s.reshape(bsz, -1, CONV_WIDTH - 1, s.shape[-1])[None, :, -1]
    return (y.reshape(bsz, t, D_MODEL), heads(kf), heads(vf), last(new_conv), last(new_ffn))


def kernel(x_prompt, x_sample, cache_k, cache_v, state_conv, state_ffn_conv, c_prompt, c_sample,
           w_ada, b_ada, g_norm1, w_in, w_conv, b_conv, w_branch_a, w_branch_b, w_out,
           g_norm2, w_up, w_fconv, b_fconv, w_down, g_final):
    assert w_ada.shape[0] == 1, "one trunk layer"
    bp, bs = x_prompt.shape[0], x_sample.shape[0]
    d_ff = w_down.shape[1]
    w = {
        "g1": g_norm1[0].reshape(1, D_MODEL),
        "w_kvt": w_in[0][:, SEG_K * D_MODEL:(SEG_V + 1) * D_MODEL].astype(BF16)
        .reshape(D_MODEL, 2, D_MODEL).transpose(1, 2, 0),
        "w_seg": w_in[0].astype(BF16),
        "w_conv": w_conv[0], "b_conv": b_conv[0].reshape(1, -1),
        "wa": w_branch_a[0].astype(BF16), "wb": w_branch_b[0].astype(BF16),
        "wo": w_out[0].astype(BF16),
        "g2": g_norm2[0].reshape(1, D_MODEL),
        "w_up2": w_up[0].astype(BF16),
        "w_fconv": w_fconv[0], "b_fconv": b_fconv[0].reshape(1, -1),
        "w_down": w_down[0].astype(BF16),
        "g_final": g_final.reshape(1, D_MODEL),
    }
    mod = _ada(jnp.concatenate([c_prompt, c_sample], axis=0), w_ada[0], b_ada[0])
    mod = mod.reshape(bp + bs, N_MOD, D_MODEL)
    zeros = lambda width: jnp.zeros((bp, CONV_WIDTH - 1, width), x_prompt.dtype)
    yp, kp, vp, cp, fp = _trunk(x_prompt, mod[:bp], None, None, zeros(D_MODEL), zeros(d_ff), w,
                                PROMPT_TILE, PROMPT_TILE_WIDE)
    n_sample = bs * x_sample.shape[1]
    ys, ks, vs, cs, fs = _trunk(x_sample, mod[bp:], cache_k[0], cache_v[0], state_conv[0],
                                state_ffn_conv[0], w, n_sample, n_sample)
    return (yp, ys, kp, vp, cp, fp, ks, vs, cs, fs)
```

```python
import functools
import math

import jax
import jax.numpy as jnp
from jax import lax
from jax.experimental import pallas as pl
from jax.experimental.pallas import tpu as pltpu

F32 = jnp.float32
BF16 = jnp.bfloat16

D_MODEL = 1024
N_HEADS = 16
HEAD_DIM = 64
N_SEG = 8
SEG_Q, SEG_K, SEG_V, SEG_B, SEG_C, SEG_U, SEG_GA, SEG_GB = range(N_SEG)
N_MOD = 6
CONV_WIDTH = 3
EPS = 1e-6
LANES = 128
SUBLANES = 8
MXU_WIDTH = 256
FFN_CHUNK = 3 * MXU_WIDTH
ATTN_BLOCK = 256
ATTN_LANE_TILES = 8
VMEM_LIMIT = 56 * 1024 * 1024
PROMPT_TILE = 512
PROMPT_TILE_WIDE = 1024
LOG2E = math.log2(math.e)
Q_SCALE = HEAD_DIM ** -0.5 * LOG2E
DEAD_LOG2_DECAY = 105.0 * LOG2E
CAUSAL_BIAS = -1e30


def _params(n_axes):
    return pltpu.CompilerParams(dimension_semantics=("arbitrary",) * n_axes,
                                vmem_limit_bytes=VMEM_LIMIT)


def _resident(shape):
    return pl.BlockSpec(shape, lambda *_: (0,) * len(shape), pipeline_mode=pl.Buffered(1))


def _sigmoid(x):
    return 1.0 / (1.0 + jnp.exp(-x))


def _rms_modulate(x, gain, shift, scale):
    y = x * lax.rsqrt(jnp.mean(x * x, axis=-1, keepdims=True) + EPS) * gain
    return y * (1.0 + scale) + shift


def _shift_rows(u, fill):
    r = pltpu.roll(u, 1, 0)
    is_row0 = lax.broadcasted_iota(jnp.int32, (SUBLANES, u.shape[1]), 0) == 0
    top = jnp.where(is_row0, fill, r[:SUBLANES])
    return jnp.concatenate([top, r[SUBLANES:]], axis=0)


def _causal_conv(u, hist, w, b):
    u1 = _shift_rows(u, hist[1:2])
    u2 = _shift_rows(u1, hist[0:1])
    return b + w[0:1] * u2 + w[1:2] * u1 + w[2:3] * u


def _conv_tile(u, sl, w_ref, b_ref, prev_ref, hist_scr, new_ref):
    n_seq = prev_ref.shape[0]
    rows = u.shape[0] // n_seq
    out = []
    for s in range(n_seq):
        part = u[s * rows:(s + 1) * rows]
        hist = hist_scr[:, sl] if n_seq == 1 else prev_ref[s, :, sl]
        out.append(_causal_conv(part, hist, w_ref[:, sl], b_ref[:, sl]))
        tail = part[rows - 2:rows]
        if n_seq == 1:
            hist_scr[:, sl] = tail
        new_ref[s, :, sl] = tail
    return out[0] if n_seq == 1 else jnp.concatenate(out, axis=0)


def _start_of_sequence(prev_ref, hist_scr, tiles_per_seq):
    if prev_ref.shape[0] == 1:
        @pl.when(pl.program_id(0) % tiles_per_seq == 0)
        def _():
            hist_scr[...] = prev_ref[0]


def _mod_rows(mod_ref, k, tm):
    n_seq = mod_ref.shape[0]
    if n_seq == 1:
        return mod_ref[0, k:k + 1, :]
    return jnp.concatenate([jnp.broadcast_to(mod_ref[s, k:k + 1, :], (tm // n_seq, D_MODEL))
                            for s in range(n_seq)], axis=0)


def _seq_tiling(seq_len, tm):
    assert tm % seq_len == 0 or seq_len % tm == 0
    return max(1, tm // seq_len), max(1, seq_len // tm)


def _ada_kernel(c_ref, w_ref, b_ref, o_ref):
    c = c_ref[...]
    o_ref[...] = jnp.dot(c * _sigmoid(c), w_ref[...], preferred_element_type=F32) + b_ref[...]


def _ada(c, w_ada, b_ada):
    n, tn = c.shape[0], 1536
    return pl.pallas_call(
        _ada_kernel,
        grid=(N_MOD * D_MODEL // tn,),
        in_specs=[pl.BlockSpec((n, D_MODEL), lambda j: (0, 0)),
                  pl.BlockSpec((D_MODEL, tn), lambda j: (0, j)),
                  pl.BlockSpec((1, tn), lambda j: (0, j))],
        out_specs=pl.BlockSpec((n, tn), lambda j: (0, j)),
        out_shape=jax.ShapeDtypeStruct((n, N_MOD * D_MODEL), F32),
        compiler_params=_params(1),
        name="ada",
    )(c, w_ada, b_ada.reshape(1, -1))


def _inproj_kernel(x_ref, mod_ref, g_ref, wt_ref, w_ref, wc_ref, bc_ref, prev_ref,
                   q_ref, kf_ref, vf_ref, kb_ref, vb_ref, ob_ref, ga_ref, gb_ref, nc_ref,
                   hist_scr, *, tiles_per_seq):
    tm = x_ref.shape[0]
    _start_of_sequence(prev_ref, hist_scr, tiles_per_seq)
    h = _rms_modulate(x_ref[...], g_ref[...], _mod_rows(mod_ref, 0, tm), _mod_rows(mod_ref, 1, tm))
    hb = h.astype(BF16)

    for c in range(D_MODEL // MXU_WIDTH):
        sl = slice(c * MXU_WIDTH, (c + 1) * MXU_WIDTH)

        def proj(seg):
            cols = slice(seg * D_MODEL + sl.start, seg * D_MODEL + sl.stop)
            return jnp.dot(hb, w_ref[:, cols], preferred_element_type=F32)

        q_ref[:, sl] = (proj(SEG_Q) * Q_SCALE).astype(BF16)
        ga_ref[:, sl] = _sigmoid(proj(SEG_GA)).astype(BF16)
        gb_ref[:, sl] = _sigmoid(proj(SEG_GB)).astype(BF16)
        cu = proj(SEG_C) * proj(SEG_U)
        conv = _conv_tile(cu, sl, wc_ref, bc_ref, prev_ref, hist_scr, nc_ref)
        ob_ref[:, sl] = (proj(SEG_B) * conv).astype(BF16)

    n_seq = kf_ref.shape[0]
    for s, f_ref, b_ref in ((0, kf_ref, kb_ref), (1, vf_ref, vb_ref)):
        t = lax.dot_general(wt_ref[s], hb, (((1,), (1,)), ((), ())),
                            preferred_element_type=F32)
        for sq in range(n_seq):
            cols = slice(sq * (tm // n_seq), (sq + 1) * (tm // n_seq))
            f_ref[sq] = t[:, cols]
            b_ref[sq] = t[:, cols].astype(BF16)


def _inproj(x2d, mod, g1, w_kvt, w_seg, w_conv, b_conv, prev, seq_len, tm):
    m = x2d.shape[0]
    n_seq = m // seq_len
    spt, tiles_per_seq = _seq_tiling(seq_len, tm)
    row = pl.BlockSpec((tm, D_MODEL), lambda i: (i, 0))
    chan = pl.BlockSpec((spt, D_MODEL, tm // spt),
                        lambda i: (i // tiles_per_seq, 0, i % tiles_per_seq))
    act = lambda dt: jax.ShapeDtypeStruct((m, D_MODEL), dt)
    act_t = lambda dt: jax.ShapeDtypeStruct((n_seq, D_MODEL, seq_len), dt)
    state = (spt, CONV_WIDTH - 1, D_MODEL)
    return pl.pallas_call(
        functools.partial(_inproj_kernel, tiles_per_seq=tiles_per_seq),
        grid=(m // tm,),
        in_specs=[row,
                  pl.BlockSpec((spt, N_MOD, D_MODEL), lambda i: (i // tiles_per_seq, 0, 0)),
                  _resident((1, D_MODEL)),
                  _resident((2, D_MODEL, D_MODEL)),
                  _resident((D_MODEL, N_SEG * D_MODEL)),
                  _resident((CONV_WIDTH, D_MODEL)),
                  _resident((1, D_MODEL)),
                  pl.BlockSpec(state, lambda i: (i // tiles_per_seq, 0, 0))],
        out_specs=[row, chan, chan, chan, chan, row, row, row,
                   pl.BlockSpec(state, lambda i: (i, 0, 0))],
        out_shape=[act(BF16), act_t(F32), act_t(F32), act_t(BF16), act_t(BF16), act(BF16),
                   act(BF16), act(BF16),
                   jax.ShapeDtypeStruct((m // tm * spt, CONV_WIDTH - 1, D_MODEL), F32)],
        scratch_shapes=[pltpu.VMEM((CONV_WIDTH - 1, D_MODEL), F32)],
        compiler_params=_params(1),
        name="inproj",
    )(x2d, mod, g1, w_kvt, w_seg, w_conv, b_conv, prev)


def _attn_block(qs, kblk, vblk, tri, causal):
    bq = qs.shape[0] // 2
    lo = lax.broadcasted_iota(jnp.int32, (1, LANES), 1) < HEAD_DIM
    z = jnp.dot(qs, kblk, preferred_element_type=F32)
    sp = jnp.maximum(z, 0.0) + jnp.log(1.0 + jnp.exp2(-jnp.abs(z))) * LOG2E
    if causal is not None:
        sp = sp * causal[0]
    s = jnp.dot(sp.astype(BF16), tri, preferred_element_type=F32)
    x = z - s
    if causal is not None:
        x = x + causal[1]
    a = jnp.exp2(x)
    pv = lax.dot_general(a.astype(BF16), vblk, (((1,), (1,)), ((), ())),
                         preferred_element_type=F32)
    tot = jnp.sum(sp, axis=1, keepdims=True)
    return jnp.where(lo, tot[:bq], tot[bq:]), jnp.where(lo, pv[:bq], pv[bq:])


def _attn_kernel(q_ref, k_ref, v_ref, tri_ref, keep_ref, bias_ref, o_ref,
                 qs_scr, acc_scr, decay_scr, *, q_off):
    bq, bk = q_ref.shape[1], ATTN_BLOCK
    n_lt = q_ref.shape[2] // LANES
    j_diag = (q_off + pl.program_id(2) * bq) // bk
    lo = lax.broadcasted_iota(jnp.int32, (1, LANES), 1) < HEAD_DIM
    lanes = [slice(lt * LANES, (lt + 1) * LANES) for lt in range(n_lt)]
    for lt in range(n_lt):
        q = q_ref[0, :, lanes[lt]]
        qs_scr[lt, :bq] = jnp.where(lo, q, jnp.zeros_like(q))
        qs_scr[lt, bq:] = jnp.where(lo, jnp.zeros_like(q), q)

    def block(lt, jb, causal):
        start = pl.multiple_of(jb * bk, bk)
        return _attn_block(qs_scr[lt], k_ref[0, lanes[lt], pl.ds(start, bk)],
                           v_ref[0, lanes[lt], pl.ds(start, bk)], tri_ref[...], causal)

    def diag_block(lt):
        return block(lt, j_diag, (keep_ref[...], bias_ref[...]))

    def any_live(decays):
        m = functools.reduce(jnp.minimum, decays)
        m = jnp.min(jnp.min(m, axis=0, keepdims=True), axis=1, keepdims=True)
        return m[0, 0] < DEAD_LOG2_DECAY

    @pl.when(j_diag == 0)
    def _():
        for lt in range(n_lt):
            _, pv = diag_block(lt)
            o_ref[0, :, lanes[lt]] = pv.astype(o_ref.dtype)

    @pl.when(j_diag > 0)
    def _():
        newest = [(diag_block(lt), block(lt, j_diag - 1, None)) for lt in range(n_lt)]
        accs = [pv0 + jnp.exp2(-tot0) * pv1 for (tot0, pv0), (_, pv1) in newest]
        decays = [tot0 + tot1 for (tot0, _), (tot1, _) in newest]
        for lt in range(n_lt):
            o_ref[0, :, lanes[lt]] = accs[lt].astype(o_ref.dtype)

        @pl.when(jnp.logical_and(j_diag >= 2, any_live(decays)))
        def _():
            for lt in range(n_lt):
                acc_scr[lt] = accs[lt]
                decay_scr[lt] = decays[lt]

            def older(state):
                jb, _ = state
                blocks = [block(lt, jb, None) for lt in range(n_lt)]
                decays = []
                for lt, (tot, pv) in enumerate(blocks):
                    decay = decay_scr[lt]
                    acc_scr[lt] += jnp.exp2(-decay) * pv
                    decays.append(decay + tot)
                    decay_scr[lt] = decays[lt]
                return jb - 1, any_live(decays)

            lax.while_loop(lambda state: jnp.logical_and(state[0] >= 0, state[1]), older,
                           (j_diag - 2, jnp.bool_(True)))
            for lt in range(n_lt):
                o_ref[0, :, lanes[lt]] = acc_scr[lt].astype(o_ref.dtype)


def _attention(q, kt, vt, q_off):
    b, tq, _ = q.shape
    tk = kt.shape[2]
    width = ATTN_LANE_TILES * LANES
    bq = min(tq, ATTN_BLOCK)
    assert bq & (bq - 1) == 0 and tq % bq == 0 and tk % ATTN_BLOCK == 0
    assert q_off % ATTN_BLOCK == 0 and (bq == ATTN_BLOCK or tq == bq)
    assert q_off + tq <= tk
    r = lax.broadcasted_iota(jnp.int32, (ATTN_BLOCK, ATTN_BLOCK), 0)
    c = lax.broadcasted_iota(jnp.int32, (ATTN_BLOCK, ATTN_BLOCK), 1)
    tri = (r >= c).astype(BF16)
    visible = jnp.tile(c[:bq] < r[:bq], (2, 1))
    keep = visible.astype(F32)
    bias = jnp.where(visible, 0.0, CAUSAL_BIAS).astype(F32)
    return pl.pallas_call(
        functools.partial(_attn_kernel, q_off=q_off),
        grid=(b, D_MODEL // width, tq // bq),
        in_specs=[pl.BlockSpec((1, bq, width), lambda bi, hg, qi: (bi, qi, hg)),
                  pl.BlockSpec((1, width, tk), lambda bi, hg, qi: (bi, hg, 0)),
                  pl.BlockSpec((1, width, tk), lambda bi, hg, qi: (bi, hg, 0)),
                  _resident((ATTN_BLOCK, ATTN_BLOCK)),
                  _resident((2 * bq, ATTN_BLOCK)),
                  _resident((2 * bq, ATTN_BLOCK))],
        out_specs=pl.BlockSpec((1, bq, width), lambda bi, hg, qi: (bi, qi, hg)),
        out_shape=jax.ShapeDtypeStruct(q.shape, BF16),
        scratch_shapes=[pltpu.VMEM((ATTN_LANE_TILES, 2 * bq, LANES), BF16),
                        pltpu.VMEM((ATTN_LANE_TILES, bq, LANES), F32),
                        pltpu.VMEM((ATTN_LANE_TILES, bq, LANES), F32)],
        compiler_params=_params(3),
        name="attn",
    )(q, kt, vt, tri, keep, bias)


def _mix_kernel(oa_ref, ob_ref, ga_ref, gb_ref, x_ref, mod_ref, wa_ref, wb_ref, wo_ref, g_ref,
                x1_ref, h2_ref):
    ya = jnp.dot(oa_ref[...], wa_ref[...], preferred_element_type=F32)
    yb = jnp.dot(ob_ref[...], wb_ref[...], preferred_element_type=F32)
    merged = ga_ref[...].astype(F32) * ya + gb_ref[...].astype(F32) * yb
    y = jnp.dot(merged.astype(BF16), wo_ref[...], preferred_element_type=F32)
    tm = x_ref.shape[0]
    x1 = x_ref[...] + _mod_rows(mod_ref, 2, tm) * y
    x1_ref[...] = x1
    h2 = _rms_modulate(x1, g_ref[...], _mod_rows(mod_ref, 3, tm), _mod_rows(mod_ref, 4, tm))
    h2_ref[...] = h2.astype(BF16)


def _mix(oa, ob, ga, gb, x2d, mod, wa, wb, wo, g2, seq_len, tm):
    m = x2d.shape[0]
    spt, tiles_per_seq = _seq_tiling(seq_len, tm)
    row = pl.BlockSpec((tm, D_MODEL), lambda i: (i, 0))
    full = _resident((D_MODEL, D_MODEL))
    return pl.pallas_call(
        _mix_kernel,
        grid=(m // tm,),
        in_specs=[row, row, row, row, row,
                  pl.BlockSpec((spt, N_MOD, D_MODEL), lambda i: (i // tiles_per_seq, 0, 0)),
                  full, full, full, _resident((1, D_MODEL))],
        out_specs=[row, row],
        out_shape=[jax.ShapeDtypeStruct((m, D_MODEL), F32), jax.ShapeDtypeStruct((m, D_MODEL), BF16)],
        compiler_params=_params(1),
        name="mix",
    )(oa, ob, ga, gb, x2d, mod, wa, wb, wo, g2)


def _ffn_kernel(h_ref, x_ref, mod_ref, wu_ref, wf_ref, bf_ref, prev_ref, wd_ref, g_ref,
                y_ref, nf_ref, act_scr, hist_scr, *, tiles_per_seq):
    tm, d_ff = act_scr.shape
    _start_of_sequence(prev_ref, hist_scr, tiles_per_seq)

    for start in range(0, d_ff, FFN_CHUNK):
        sl = slice(start, min(start + FFN_CHUNK, d_ff))
        up = jnp.dot(h_ref[...], wu_ref[:, sl], preferred_element_type=F32)
        gate = jnp.dot(h_ref[...], wu_ref[:, slice(d_ff + sl.start, d_ff + sl.stop)],
                       preferred_element_type=F32)
        up_c = _conv_tile(up, sl, wf_ref, bf_ref, prev_ref, hist_scr, nf_ref)
        act_scr[:, sl] = (up_c * _sigmoid(up_c) * gate).astype(BF16)

    y = jnp.dot(act_scr[...], wd_ref[...], preferred_element_type=F32)
    x2 = x_ref[...] + _mod_rows(mod_ref, 5, tm) * y
    y_ref[...] = x2 * lax.rsqrt(jnp.mean(x2 * x2, axis=-1, keepdims=True) + EPS) * g_ref[...]


def _ffn(h2, x1, mod, w_up2, w_fconv, b_fconv, prev, w_down, g_final, seq_len, tm):
    m = x1.shape[0]
    d_ff = w_down.shape[0]
    assert d_ff % MXU_WIDTH == 0
    spt, tiles_per_seq = _seq_tiling(seq_len, tm)
    row = pl.BlockSpec((tm, D_MODEL), lambda i: (i, 0))
    state = (spt, CONV_WIDTH - 1, d_ff)
    return pl.pallas_call(
        functools.partial(_ffn_kernel, tiles_per_seq=tiles_per_seq),
        grid=(m // tm,),
        in_specs=[row, row,
                  pl.BlockSpec((spt, N_MOD, D_MODEL), lambda i: (i // tiles_per_seq, 0, 0)),
                  _resident((D_MODEL, 2 * d_ff)),
                  _resident((CONV_WIDTH, d_ff)),
                  _resident((1, d_ff)),
                  pl.BlockSpec(state, lambda i: (i // tiles_per_seq, 0, 0)),
                  _resident((d_ff, D_MODEL)),
                  _resident((1, D_MODEL))],
        out_specs=[row, pl.BlockSpec(state, lambda i: (i, 0, 0))],
        out_shape=[jax.ShapeDtypeStruct((m, D_MODEL), F32),
                   jax.ShapeDtypeStruct((m // tm * spt, CONV_WIDTH - 1, d_ff), F32)],
        scratch_shapes=[pltpu.VMEM((tm, d_ff), BF16),
                        pltpu.VMEM((CONV_WIDTH - 1, d_ff), F32)],
        compiler_params=_params(1),
        name="ffn",
    )(h2, x1, mod, w_up2, w_fconv, b_fconv, prev, w_down, g_final)


def _trunk(x, mod, past_k, past_v, conv_prev, ffn_prev, w, tm, tm_wide):
    bsz, t, _ = x.shape
    x2d = x.reshape(bsz * t, D_MODEL)
    q, kf, vf, kb, vb, ob, ga, gb, new_conv = _inproj(
        x2d, mod, w["g1"], w["w_kvt"], w["w_seg"], w["w_conv"], w["b_conv"], conv_prev, t, tm)
    if past_k is None:
        k_all, v_all, q_off = kb, vb, 0
    else:
        q_off = past_k.shape[1]
        pad = -(q_off + t) % ATTN_BLOCK
        cat = lambda past, new: jnp.pad(
            jnp.concatenate([past.transpose(0, 2, 3, 1).reshape(bsz, D_MODEL, q_off).astype(BF16),
                             new], axis=2),
            ((0, 0), (0, 0), (0, pad)))
        k_all, v_all = cat(past_k, kb), cat(past_v, vb)
    oa = _attention(q.reshape(bsz, t, D_MODEL), k_all, v_all, q_off).reshape(bsz * t, D_MODEL)
    x1, h2 = _mix(oa, ob, ga, gb, x2d, mod, w["wa"], w["wb"], w["wo"], w["g2"], t, tm_wide)
    y, new_ffn = _ffn(h2, x1, mod, w["w_up2"], w["w_fconv"], w["b_fconv"], ffn_prev, w["w_down"],
                      w["g_final"], t, tm_wide)
    heads = lambda a: a.reshape(bsz, N_HEADS, HEAD_DIM, t).transpose(0, 3, 1, 2)[None]
    last = lambda s: s.reshape(bsz, -1, CONV_WIDTH - 1, s.shape[-1])[None, :, -1]
    return (y.reshape(bsz, t, D_MODEL), heads(kf), heads(vf), last(new_conv), last(new_ffn))


def kernel(x_prompt, x_sample, cache_k, cache_v, state_conv, state_ffn_conv, c_prompt, c_sample,
           w_ada, b_ada, g_norm1, w_in, w_conv, b_conv, w_branch_a, w_branch_b, w_out,
           g_norm2, w_up, w_fconv, b_fconv, w_down, g_final):
    assert w_ada.shape[0] == 1, "one trunk layer"
    bp, bs = x_prompt.shape[0], x_sample.shape[0]
    d_ff = w_down.shape[1]
    w = {
        "g1": g_norm1[0].reshape(1, D_MODEL),
        "w_kvt": w_in[0][:, SEG_K * D_MODEL:(SEG_V + 1) * D_MODEL].astype(BF16)
        .reshape(D_MODEL, 2, D_MODEL).transpose(1, 2, 0),
        "w_seg": w_in[0].astype(BF16),
        "w_conv": w_conv[0], "b_conv": b_conv[0].reshape(1, -1),
        "wa": w_branch_a[0].astype(BF16), "wb": w_branch_b[0].astype(BF16),
        "wo": w_out[0].astype(BF16),
        "g2": g_norm2[0].reshape(1, D_MODEL),
        "w_up2": w_up[0].astype(BF16),
        "w_fconv": w_fconv[0], "b_fconv": b_fconv[0].reshape(1, -1),
        "w_down": w_down[0].astype(BF16),
        "g_final": g_final.reshape(1, D_MODEL),
    }
    mod = _ada(jnp.concatenate([c_prompt, c_sample], axis=0), w_ada[0], b_ada[0])
    mod = mod.reshape(bp + bs, N_MOD, D_MODEL)
    zeros = lambda width: jnp.zeros((bp, CONV_WIDTH - 1, width), x_prompt.dtype)
    yp, kp, vp, cp, fp = _trunk(x_prompt, mod[:bp], None, None, zeros(D_MODEL), zeros(d_ff), w,
                                PROMPT_TILE, PROMPT_TILE_WIDE)
    n_sample = bs * x_sample.shape[1]
    ys, ks, vs, cs, fs = _trunk(x_sample, mod[bp:], cache_k[0], cache_v[0], state_conv[0],
                                state_ffn_conv[0], w, n_sample, n_sample)
    return (yp, ys, kp, vp, cp, fp, ks, vs, cs, fs)
```

```python
import functools
import math

import jax
import jax.numpy as jnp
from jax import lax
from jax.experimental import pallas as pl
from jax.experimental.pallas import tpu as pltpu

F32 = jnp.float32
BF16 = jnp.bfloat16

D_MODEL = 1024
N_HEADS = 16
HEAD_DIM = 64
N_SEG = 8
SEG_Q, SEG_K, SEG_V, SEG_B, SEG_C, SEG_U, SEG_GA, SEG_GB = range(N_SEG)
N_MOD = 6
CONV_WIDTH = 3
EPS = 1e-6
LANES = 128
SUBLANES = 8
MXU_WIDTH = 256
FFN_CHUNK = 3 * MXU_WIDTH
MIX_CHUNK = MXU_WIDTH
ATTN_BLOCK = 256
ATTN_LANE_TILES = 8
VMEM_LIMIT = 56 * 1024 * 1024
PROMPT_TILE = 512
PROMPT_TILE_WIDE = 1024
LOG2E = math.log2(math.e)
Q_SCALE = HEAD_DIM ** -0.5 * LOG2E
DEAD_LOG2_DECAY = 105.0 * LOG2E
CAUSAL_BIAS = -1e30


def _params(n_axes):
    return pltpu.CompilerParams(dimension_semantics=("arbitrary",) * n_axes,
                                vmem_limit_bytes=VMEM_LIMIT)


def _resident(shape):
    return pl.BlockSpec(shape, lambda *_: (0,) * len(shape), pipeline_mode=pl.Buffered(1))


def _sigmoid(x):
    return 1.0 / (1.0 + jnp.exp(-x))


def _rms_modulate(x, gain, shift, scale):
    y = x * lax.rsqrt(jnp.mean(x * x, axis=-1, keepdims=True) + EPS) * gain
    return y * (1.0 + scale) + shift


def _shift_rows(u, fill):
    r = pltpu.roll(u, 1, 0)
    is_row0 = lax.broadcasted_iota(jnp.int32, (SUBLANES, u.shape[1]), 0) == 0
    top = jnp.where(is_row0, fill, r[:SUBLANES])
    return jnp.concatenate([top, r[SUBLANES:]], axis=0)


def _causal_conv(u, hist, w, b):
    u1 = _shift_rows(u, hist[1:2])
    u2 = _shift_rows(u1, hist[0:1])
    return b + w[0:1] * u2 + w[1:2] * u1 + w[2:3] * u


def _conv_tile(u, sl, w_ref, b_ref, prev_ref, hist_scr, new_ref):
    n_seq = prev_ref.shape[0]
    rows = u.shape[0] // n_seq
    out = []
    for s in range(n_seq):
        part = u[s * rows:(s + 1) * rows]
        hist = hist_scr[:, sl] if n_seq == 1 else prev_ref[s, :, sl]
        out.append(_causal_conv(part, hist, w_ref[:, sl], b_ref[:, sl]))
        tail = part[rows - 2:rows]
        if n_seq == 1:
            hist_scr[:, sl] = tail
        new_ref[s, :, sl] = tail
    return out[0] if n_seq == 1 else jnp.concatenate(out, axis=0)


def _start_of_sequence(prev_ref, hist_scr, tiles_per_seq):
    if prev_ref.shape[0] == 1:
        @pl.when(pl.program_id(0) % tiles_per_seq == 0)
        def _():
            hist_scr[...] = prev_ref[0]


def _mod_rows(mod_ref, k, tm):
    n_seq = mod_ref.shape[0]
    if n_seq == 1:
        return mod_ref[0, k:k + 1, :]
    return jnp.concatenate([jnp.broadcast_to(mod_ref[s, k:k + 1, :], (tm // n_seq, D_MODEL))
                            for s in range(n_seq)], axis=0)


def _seq_tiling(seq_len, tm):
    assert tm % seq_len == 0 or seq_len % tm == 0
    return max(1, tm // seq_len), max(1, seq_len // tm)


def _ada_kernel(c_ref, w_ref, b_ref, o_ref):
    c = c_ref[...]
    o_ref[...] = jnp.dot(c * _sigmoid(c), w_ref[...], preferred_element_type=F32) + b_ref[...]


def _ada(c, w_ada, b_ada):
    n, tn = c.shape[0], 1536
    return pl.pallas_call(
        _ada_kernel,
        grid=(N_MOD * D_MODEL // tn,),
        in_specs=[pl.BlockSpec((n, D_MODEL), lambda j: (0, 0)),
                  pl.BlockSpec((D_MODEL, tn), lambda j: (0, j)),
                  pl.BlockSpec((1, tn), lambda j: (0, j))],
        out_specs=pl.BlockSpec((n, tn), lambda j: (0, j)),
        out_shape=jax.ShapeDtypeStruct((n, N_MOD * D_MODEL), F32),
        compiler_params=_params(1),
        name="ada",
    )(c, w_ada, b_ada.reshape(1, -1))


def _inproj_kernel(x_ref, mod_ref, g_ref, wt_ref, w_ref, wc_ref, bc_ref, prev_ref,
                   q_ref, kf_ref, vf_ref, kb_ref, vb_ref, ob_ref, ga_ref, gb_ref, nc_ref,
                   hist_scr, *, tiles_per_seq):
    tm = x_ref.shape[0]
    _start_of_sequence(prev_ref, hist_scr, tiles_per_seq)
    h = _rms_modulate(x_ref[...], g_ref[...], _mod_rows(mod_ref, 0, tm), _mod_rows(mod_ref, 1, tm))
    hb = h.astype(BF16)

    for c in range(D_MODEL // MXU_WIDTH):
        sl = slice(c * MXU_WIDTH, (c + 1) * MXU_WIDTH)

        def proj(seg):
            cols = slice(seg * D_MODEL + sl.start, seg * D_MODEL + sl.stop)
            return jnp.dot(hb, w_ref[:, cols], preferred_element_type=F32)

        q_ref[:, sl] = (proj(SEG_Q) * Q_SCALE).astype(BF16)
        ga_ref[:, sl] = _sigmoid(proj(SEG_GA)).astype(BF16)
        gb_ref[:, sl] = _sigmoid(proj(SEG_GB)).astype(BF16)
        cu = proj(SEG_C) * proj(SEG_U)
        conv = _conv_tile(cu, sl, wc_ref, bc_ref, prev_ref, hist_scr, nc_ref)
        ob_ref[:, sl] = (proj(SEG_B) * conv).astype(BF16)

    n_seq = kf_ref.shape[0]
    for s, f_ref, b_ref in ((0, kf_ref, kb_ref), (1, vf_ref, vb_ref)):
        t = lax.dot_general(wt_ref[s], hb, (((1,), (1,)), ((), ())),
                            preferred_element_type=F32)
        for sq in range(n_seq):
            cols = slice(sq * (tm // n_seq), (sq + 1) * (tm // n_seq))
            f_ref[sq] = t[:, cols]
            b_ref[sq] = t[:, cols].astype(BF16)


def _inproj(x2d, mod, g1, w_kvt, w_seg, w_conv, b_conv, prev, seq_len, tm):
    m = x2d.shape[0]
    n_seq = m // seq_len
    spt, tiles_per_seq = _seq_tiling(seq_len, tm)
    row = pl.BlockSpec((tm, D_MODEL), lambda i: (i, 0))
    chan = pl.BlockSpec((spt, D_MODEL, tm // spt),
                        lambda i: (i // tiles_per_seq, 0, i % tiles_per_seq))
    act = lambda dt: jax.ShapeDtypeStruct((m, D_MODEL), dt)
    act_t = lambda dt: jax.ShapeDtypeStruct((n_seq, D_MODEL, seq_len), dt)
    state = (spt, CONV_WIDTH - 1, D_MODEL)
    return pl.pallas_call(
        functools.partial(_inproj_kernel, tiles_per_seq=tiles_per_seq),
        grid=(m // tm,),
        in_specs=[row,
                  pl.BlockSpec((spt, N_MOD, D_MODEL), lambda i: (i // tiles_per_seq, 0, 0)),
                  _resident((1, D_MODEL)),
                  _resident((2, D_MODEL, D_MODEL)),
                  _resident((D_MODEL, N_SEG * D_MODEL)),
                  _resident((CONV_WIDTH, D_MODEL)),
                  _resident((1, D_MODEL)),
                  pl.BlockSpec(state, lambda i: (i // tiles_per_seq, 0, 0))],
        out_specs=[row, chan, chan, chan, chan, row, row, row,
                   pl.BlockSpec(state, lambda i: (i, 0, 0))],
        out_shape=[act(BF16), act_t(F32), act_t(F32), act_t(BF16), act_t(BF16), act(BF16),
                   act(BF16), act(BF16),
                   jax.ShapeDtypeStruct((m // tm * spt, CONV_WIDTH - 1, D_MODEL), F32)],
        scratch_shapes=[pltpu.VMEM((CONV_WIDTH - 1, D_MODEL), F32)],
        compiler_params=_params(1),
        name="inproj",
    )(x2d, mod, g1, w_kvt, w_seg, w_conv, b_conv, prev)


def _attn_block(qs, kblk, vblk, tri, causal):
    bq = qs.shape[0] // 2
    lo = lax.broadcasted_iota(jnp.int32, (1, LANES), 1) < HEAD_DIM
    z = jnp.dot(qs, kblk, preferred_element_type=F32)
    sp = jnp.maximum(z, 0.0) + jnp.log(1.0 + jnp.exp2(-jnp.abs(z))) * LOG2E
    if causal is not None:
        sp = sp * causal[0]
    s = jnp.dot(sp.astype(BF16), tri, preferred_element_type=F32)
    x = z - s
    if causal is not None:
        x = x + causal[1]
    a = jnp.exp2(x)
    pv = lax.dot_general(a.astype(BF16), vblk, (((1,), (1,)), ((), ())),
                         preferred_element_type=F32)
    tot = jnp.sum(sp, axis=1, keepdims=True)
    return jnp.where(lo, tot[:bq], tot[bq:]), jnp.where(lo, pv[:bq], pv[bq:])


def _attn_kernel(q_ref, k_ref, v_ref, tri_ref, keep_ref, bias_ref, o_ref,
                 qs_scr, acc_scr, decay_scr, *, q_off):
    bq, bk = q_ref.shape[1], ATTN_BLOCK
    n_lt = q_ref.shape[2] // LANES
    j_diag = (q_off + pl.program_id(2) * bq) // bk
    lo = lax.broadcasted_iota(jnp.int32, (1, LANES), 1) < HEAD_DIM
    lanes = [slice(lt * LANES, (lt + 1) * LANES) for lt in range(n_lt)]
    for lt in range(n_lt):
        q = q_ref[0, :, lanes[lt]]
        qs_scr[lt, :bq] = jnp.where(lo, q, jnp.zeros_like(q))
        qs_scr[lt, bq:] = jnp.where(lo, jnp.zeros_like(q), q)

    def block(lt, jb, causal):
        start = pl.multiple_of(jb * bk, bk)
        return _attn_block(qs_scr[lt], k_ref[0, lanes[lt], pl.ds(start, bk)],
                           v_ref[0, lanes[lt], pl.ds(start, bk)], tri_ref[...], causal)

    def diag_block(lt):
        return block(lt, j_diag, (keep_ref[...], bias_ref[...]))

    def any_live(decays):
        m = functools.reduce(jnp.minimum, decays)
        m = jnp.min(jnp.min(m, axis=0, keepdims=True), axis=1, keepdims=True)
        return m[0, 0] < DEAD_LOG2_DECAY

    @pl.when(j_diag == 0)
    def _():
        for lt in range(n_lt):
            _, pv = diag_block(lt)
            o_ref[0, :, lanes[lt]] = pv.astype(o_ref.dtype)

    @pl.when(j_diag > 0)
    def _():
        newest = [(diag_block(lt), block(lt, j_diag - 1, None)) for lt in range(n_lt)]
        accs = [pv0 + jnp.exp2(-tot0) * pv1 for (tot0, pv0), (_, pv1) in newest]
        decays = [tot0 + tot1 for (tot0, _), (tot1, _) in newest]
        for lt in range(n_lt):
            o_ref[0, :, lanes[lt]] = accs[lt].astype(o_ref.dtype)

        @pl.when(jnp.logical_and(j_diag >= 2, any_live(decays)))
        def _():
            for lt in range(n_lt):
                acc_scr[lt] = accs[lt]
                decay_scr[lt] = decays[lt]

            def older(state):
                jb, _ = state
                blocks = [block(lt, jb, None) for lt in range(n_lt)]
                decays = []
                for lt, (tot, pv) in enumerate(blocks):
                    decay = decay_scr[lt]
                    acc_scr[lt] += jnp.exp2(-decay) * pv
                    decays.append(decay + tot)
                    decay_scr[lt] = decays[lt]
                return jb - 1, any_live(decays)

            lax.while_loop(lambda state: jnp.logical_and(state[0] >= 0, state[1]), older,
                           (j_diag - 2, jnp.bool_(True)))
            for lt in range(n_lt):
                o_ref[0, :, lanes[lt]] = acc_scr[lt].astype(o_ref.dtype)


def _attention(q, kt, vt, q_off):
    b, tq, _ = q.shape
    tk = kt.shape[2]
    width = ATTN_LANE_TILES * LANES
    bq = min(tq, ATTN_BLOCK)
    assert bq & (bq - 1) == 0 and tq % bq == 0 and tk % ATTN_BLOCK == 0
    assert q_off % ATTN_BLOCK == 0 and (bq == ATTN_BLOCK or tq == bq)
    assert q_off + tq <= tk
    r = lax.broadcasted_iota(jnp.int32, (ATTN_BLOCK, ATTN_BLOCK), 0)
    c = lax.broadcasted_iota(jnp.int32, (ATTN_BLOCK, ATTN_BLOCK), 1)
    tri = (r >= c).astype(BF16)
    visible = jnp.tile(c[:bq] < r[:bq], (2, 1))
    keep = visible.astype(F32)
    bias = jnp.where(visible, 0.0, CAUSAL_BIAS).astype(F32)
    return pl.pallas_call(
        functools.partial(_attn_kernel, q_off=q_off),
        grid=(b, D_MODEL // width, tq // bq),
        in_specs=[pl.BlockSpec((1, bq, width), lambda bi, hg, qi: (bi, qi, hg)),
                  pl.BlockSpec((1, width, tk), lambda bi, hg, qi: (bi, hg, 0)),
                  pl.BlockSpec((1, width, tk), lambda bi, hg, qi: (bi, hg, 0)),
                  _resident((ATTN_BLOCK, ATTN_BLOCK)),
                  _resident((2 * bq, ATTN_BLOCK)),
                  _resident((2 * bq, ATTN_BLOCK))],
        out_specs=pl.BlockSpec((1, bq, width), lambda bi, hg, qi: (bi, qi, hg)),
        out_shape=jax.ShapeDtypeStruct(q.shape, BF16),
        scratch_shapes=[pltpu.VMEM((ATTN_LANE_TILES, 2 * bq, LANES), BF16),
                        pltpu.VMEM((ATTN_LANE_TILES, bq, LANES), F32),
                        pltpu.VMEM((ATTN_LANE_TILES, bq, LANES), F32)],
        compiler_params=_params(3),
        name="attn",
    )(q, kt, vt, tri, keep, bias)


def _mix_kernel(oa_ref, ob_ref, ga_ref, gb_ref, x_ref, mod_ref, wa_ref, wb_ref, wo_ref, g_ref,
                x1_ref, h2_ref, merged_scr):
    for start in range(0, D_MODEL, MIX_CHUNK):
        sl = slice(start, start + MIX_CHUNK)
        ya = jnp.dot(oa_ref[...], wa_ref[:, sl], preferred_element_type=F32)
        yb = jnp.dot(ob_ref[...], wb_ref[:, sl], preferred_element_type=F32)
        merged = ga_ref[:, sl].astype(F32) * ya + gb_ref[:, sl].astype(F32) * yb
        merged_scr[:, sl] = merged.astype(BF16)
    y = jnp.dot(merged_scr[...], wo_ref[...], preferred_element_type=F32)
    tm = x_ref.shape[0]
    x1 = x_ref[...] + _mod_rows(mod_ref, 2, tm) * y
    x1_ref[...] = x1
    h2 = _rms_modulate(x1, g_ref[...], _mod_rows(mod_ref, 3, tm), _mod_rows(mod_ref, 4, tm))
    h2_ref[...] = h2.astype(BF16)


def _mix(oa, ob, ga, gb, x2d, mod, wa, wb, wo, g2, seq_len, tm):
    m = x2d.shape[0]
    spt, tiles_per_seq = _seq_tiling(seq_len, tm)
    row = pl.BlockSpec((tm, D_MODEL), lambda i: (i, 0))
    full = _resident((D_MODEL, D_MODEL))
    return pl.pallas_call(
        _mix_kernel,
        grid=(m // tm,),
        in_specs=[row, row, row, row, row,
                  pl.BlockSpec((spt, N_MOD, D_MODEL), lambda i: (i // tiles_per_seq, 0, 0)),
                  full, full, full, _resident((1, D_MODEL))],
        out_specs=[row, row],
        out_shape=[jax.ShapeDtypeStruct((m, D_MODEL), F32), jax.ShapeDtypeStruct((m, D_MODEL), BF16)],
        scratch_shapes=[pltpu.VMEM((tm, D_MODEL), BF16)],
        compiler_params=_params(1),
        name="mix",
    )(oa, ob, ga, gb, x2d, mod, wa, wb, wo, g2)


def _ffn_kernel(h_ref, x_ref, mod_ref, wu_ref, wf_ref, bf_ref, prev_ref, wd_ref, g_ref,
                y_ref, nf_ref, act_scr, hist_scr, *, tiles_per_seq):
    tm, d_ff = act_scr.shape
    _start_of_sequence(prev_ref, hist_scr, tiles_per_seq)

    for start in range(0, d_ff, FFN_CHUNK):
        sl = slice(start, min(start + FFN_CHUNK, d_ff))
        up = jnp.dot(h_ref[...], wu_ref[:, sl], preferred_element_type=F32)
        gate = jnp.dot(h_ref[...], wu_ref[:, slice(d_ff + sl.start, d_ff + sl.stop)],
                       preferred_element_type=F32)
        up_c = _conv_tile(up, sl, wf_ref, bf_ref, prev_ref, hist_scr, nf_ref)
        act_scr[:, sl] = (up_c * _sigmoid(up_c) * gate).astype(BF16)

    y = jnp.dot(act_scr[...], wd_ref[...], preferred_element_type=F32)
    x2 = x_ref[...] + _mod_rows(mod_ref, 5, tm) * y
    y_ref[...] = x2 * lax.rsqrt(jnp.mean(x2 * x2, axis=-1, keepdims=True) + EPS) * g_ref[...]


def _ffn(h2, x1, mod, w_up2, w_fconv, b_fconv, prev, w_down, g_final, seq_len, tm):
    m = x1.shape[0]
    d_ff = w_down.shape[0]
    assert d_ff % MXU_WIDTH == 0
    spt, tiles_per_seq = _seq_tiling(seq_len, tm)
    row = pl.BlockSpec((tm, D_MODEL), lambda i: (i, 0))
    state = (spt, CONV_WIDTH - 1, d_ff)
    return pl.pallas_call(
        functools.partial(_ffn_kernel, tiles_per_seq=tiles_per_seq),
        grid=(m // tm,),
        in_specs=[row, row,
                  pl.BlockSpec((spt, N_MOD, D_MODEL), lambda i: (i // tiles_per_seq, 0, 0)),
                  _resident((D_MODEL, 2 * d_ff)),
                  _resident((CONV_WIDTH, d_ff)),
                  _resident((1, d_ff)),
                  pl.BlockSpec(state, lambda i: (i // tiles_per_seq, 0, 0)),
                  _resident((d_ff, D_MODEL)),
                  _resident((1, D_MODEL))],
        out_specs=[row, pl.BlockSpec(state, lambda i: (i, 0, 0))],
        out_shape=[jax.ShapeDtypeStruct((m, D_MODEL), F32),
                   jax.ShapeDtypeStruct((m // tm * spt, CONV_WIDTH - 1, d_ff), F32)],
        scratch_shapes=[pltpu.VMEM((tm, d_ff), BF16),
                        pltpu.VMEM((CONV_WIDTH - 1, d_ff), F32)],
        compiler_params=_params(1),
        name="ffn",
    )(h2, x1, mod, w_up2, w_fconv, b_fconv, prev, w_down, g_final)


def _trunk(x, mod, past_k, past_v, conv_prev, ffn_prev, w, tm, tm_wide):
    bsz, t, _ = x.shape
    x2d = x.reshape(bsz * t, D_MODEL)
    q, kf, vf, kb, vb, ob, ga, gb, new_conv = _inproj(
        x2d, mod, w["g1"], w["w_kvt"], w["w_seg"], w["w_conv"], w["b_conv"], conv_prev, t, tm)
    if past_k is None:
        k_all, v_all, q_off = kb, vb, 0
    else:
        q_off = past_k.shape[1]
        pad = -(q_off + t) % ATTN_BLOCK
        cat = lambda past, new: jnp.pad(
            jnp.concatenate([past.transpose(0, 2, 3, 1).reshape(bsz, D_MODEL, q_off).astype(BF16),
                             new], axis=2),
            ((0, 0), (0, 0), (0, pad)))
        k_all, v_all = cat(past_k, kb), cat(past_v, vb)
    oa = _attention(q.reshape(bsz, t, D_MODEL), k_all, v_all, q_off).reshape(bsz * t, D_MODEL)
    x1, h2 = _mix(oa, ob, ga, gb, x2d, mod, w["wa"], w["wb"], w["wo"], w["g2"], t, tm_wide)
    y, new_ffn = _ffn(h2, x1, mod, w["w_up2"], w["w_fconv"], w["b_fconv"], ffn_prev, w["w_down"],
                      w["g_final"], t, tm_wide)
    heads = lambda a: a.reshape(bsz, N_HEADS, HEAD_DIM, t).transpose(0, 3, 1, 2)[None]
    last = lambda s: s.reshape(bsz, -1, CONV_WIDTH - 1, s.shape[-1])[None, :, -1]
    return (y.reshape(bsz, t, D_MODEL), heads(kf), heads(vf), last(new_conv), last(new_ffn))


def kernel(x_prompt, x_sample, cache_k, cache_v, state_conv, state_ffn_conv, c_prompt, c_sample,
           w_ada, b_ada, g_norm1, w_in, w_conv, b_conv, w_branch_a, w_branch_b, w_out,
           g_norm2, w_up, w_fconv, b_fconv, w_down, g_final):
    assert w_ada.shape[0] == 1, "one trunk layer"
    bp, bs = x_prompt.shape[0], x_sample.shape[0]
    d_ff = w_down.shape[1]
    w = {
        "g1": g_norm1[0].reshape(1, D_MODEL),
        "w_kvt": w_in[0][:, SEG_K * D_MODEL:(SEG_V + 1) * D_MODEL].astype(BF16)
        .reshape(D_MODEL, 2, D_MODEL).transpose(1, 2, 0),
        "w_seg": w_in[0].astype(BF16),
        "w_conv": w_conv[0], "b_conv": b_conv[0].reshape(1, -1),
        "wa": w_branch_a[0].astype(BF16), "wb": w_branch_b[0].astype(BF16),
        "wo": w_out[0].astype(BF16),
        "g2": g_norm2[0].reshape(1, D_MODEL),
        "w_up2": w_up[0].astype(BF16),
        "w_fconv": w_fconv[0], "b_fconv": b_fconv[0].reshape(1, -1),
        "w_down": w_down[0].astype(BF16),
        "g_final": g_final.reshape(1, D_MODEL),
    }
    mod = _ada(jnp.concatenate([c_prompt, c_sample], axis=0), w_ada[0], b_ada[0])
    mod = mod.reshape(bp + bs, N_MOD, D_MODEL)
    zeros = lambda width: jnp.zeros((bp, CONV_WIDTH - 1, width), x_prompt.dtype)
    yp, kp, vp, cp, fp = _trunk(x_prompt, mod[:bp], None, None, zeros(D_MODEL), zeros(d_ff), w,
                                PROMPT_TILE, PROMPT_TILE_WIDE)
    n_sample = bs * x_sample.shape[1]
    ys, ks, vs, cs, fs = _trunk(x_sample, mod[bp:], cache_k[0], cache_v[0], state_conv[0],
                                state_ffn_conv[0], w, n_sample, n_sample)
    return (yp, ys, kp, vp, cp, fp, ks, vs, cs, fs)
```
